```python
import math
import jax, jax.numpy as jnp
from jax import lax
import numpy as np

D_MODEL = 1024
BATCH = 1
SEQ = 16384
DEPTH = 2
DEC_BATCH = 8
DEC_SEQ = 8192
PAST_LEN = 128

N_MIXERS = 2
N_RET_LAYERS = (DEPTH + 1) // 2
N_ATT_LAYERS = DEPTH // 2
RET_HEADS = 4
RET_QK_DIM = D_MODEL // RET_HEADS
RET_V_WIDTH = 2 * D_MODEL
RET_V_DIM = RET_V_WIDTH // RET_HEADS
RET_IN = 2 * D_MODEL + 2 * RET_V_WIDTH
RET_CHUNK = 128
ATT_HEADS = 16
ATT_HEAD_DIM = D_MODEL // ATT_HEADS
DIL_PATTERNS = ((128, 1), (512, 4), (2048, 16))
N_EXPERTS = 16
EC_CAPACITY_FACTOR = 2
D_EXPERT = 2816
RMS_EPS = 1e-6
NEG_INF = -1e30

kernel_name = "hybrid_retention_dilated_attn_ec_moe_encoder"


def rmsnorm(x, g):
    xf = x.astype(jnp.float32)
    y = xf * lax.rsqrt(jnp.mean(xf * xf, axis=-1, keepdims=True) + RMS_EPS) * g.astype(jnp.float32)
    return y.astype(x.dtype)


def alibi_slopes(n):
    return jnp.exp2(-8.0 * jnp.arange(1, n + 1, dtype=jnp.float32) / n)


def retention_direction(q, k, v, log_gamma, include_diag):
    b, S, H, dk = q.shape
    dv = v.shape[-1]
    C = RET_CHUNK
    n = S // C

    def chunks(t):
        return t.reshape(b, n, C, H, t.shape[-1]).transpose(1, 0, 3, 2, 4)

    pos = jnp.arange(C, dtype=jnp.float32)
    rel = pos[:, None] - pos[None, :]
    mask = (rel >= 0) if include_diag else (rel > 0)
    lg = log_gamma.astype(jnp.float32)
    intra = jnp.where(mask[None], jnp.exp(lg[:, None, None] * jnp.maximum(rel, 0.0)[None]), 0.0)
    q_dec = jnp.exp(lg[:, None] * (pos + 1.0)[None])[..., None]
    k_dec = jnp.exp(lg[:, None] * (C - 1.0 - pos)[None])[..., None]
    chunk_dec = jnp.exp(lg * C)[:, None, None]

    def step(state, inp):
        qc, kc, vc = inp
        inner = jnp.einsum('bhid,bhjd->bhij', qc, kc) * intra
        y = (jnp.einsum('bhij,bhje->bhie', inner, vc)
             + jnp.einsum('bhid,bhde->bhie', qc * q_dec, state))
        state = state * chunk_dec + jnp.einsum('bhjd,bhje->bhde', kc * k_dec, vc)
        return state, y

    state0 = jnp.zeros((b, H, dk, dv), jnp.float32)
    _, ys = lax.scan(step, state0, (chunks(q), chunks(k), chunks(v)))
    return ys.transpose(1, 0, 3, 2, 4).reshape(b, S, H, dv)


def retention_mixer(h, w_in, w_out, decay_fwd, decay_bwd):
    b, S, _ = h.shape
    proj = h @ w_in
    q = proj[..., :D_MODEL]
    k = proj[..., D_MODEL:2 * D_MODEL]
    v = proj[..., 2 * D_MODEL:2 * D_MODEL + RET_V_WIDTH]
    g = proj[..., 2 * D_MODEL + RET_V_WIDTH:]
    q = q.reshape(b, S, RET_HEADS, RET_QK_DIM).astype(jnp.float32)
    k = k.reshape(b, S, RET_HEADS, RET_QK_DIM).astype(jnp.float32) * (RET_QK_DIM ** -0.5)
    v = v.reshape(b, S, RET_HEADS, RET_V_DIM).astype(jnp.float32)
    lg_f = -jnp.exp(decay_fwd.astype(jnp.float32))
    lg_b = -jnp.exp(decay_bwd.astype(jnp.float32))
    y_f = retention_direction(q, k, v, lg_f, True)
    flip = lambda t: jnp.flip(t, axis=1)
    y_b = flip(retention_direction(flip(q), flip(k), flip(v), lg_b, False))
    y = y_f + y_b
    y = y * lax.rsqrt(jnp.mean(y * y, axis=-1, keepdims=True) + RMS_EPS)
    y = y.reshape(b, S, RET_V_WIDTH).astype(h.dtype)
    return (jax.nn.silu(g) * y) @ w_out


def dilated_branch(q, k, v, slopes, window, dil):
    b, S, H, dh = q.shape
    half = window // (2 * dil)
    blk = half
    L = S // dil
    nb = -(-L // blk)
    Lp = nb * blk
    Bp = b * dil

    def to_sub(t):
        return t.reshape(b, L, dil, H, dh).transpose(0, 2, 1, 3, 4).reshape(Bp, L, H, dh)

    qs, ks, vs = to_sub(q), to_sub(k), to_sub(v)
    qb = jnp.pad(qs, ((0, 0), (0, Lp - L), (0, 0), (0, 0))).reshape(Bp, nb, blk, H, dh)

    def windows(t):
        tp = jnp.pad(t, ((0, 0), (blk, Lp - L + blk), (0, 0), (0, 0))).reshape(Bp, nb + 2, blk, H, dh)
        return jnp.concatenate([tp[:, :nb], tp[:, 1:nb + 1], tp[:, 2:nb + 2]], axis=2)

    kw, vw = windows(ks), windows(vs)
    qi = jnp.arange(blk)
    ki = jnp.arange(3 * blk) - blk
    rel = ki[None, :] - qi[:, None]
    kpos = jnp.arange(nb)[:, None] * blk + ki[None, :]
    valid = (jnp.abs(rel) <= half)[None] & ((kpos >= 0) & (kpos < L))[:, None, :]
    bias = -slopes[:, None, None] * (dil * jnp.abs(rel)).astype(jnp.float32)[None]

    s = jnp.einsum('bnqhd,bnkhd->bnhqk', qb, kw) * (dh ** -0.5) + bias[None, None]
    s = jnp.where(valid[None, :, None], s, NEG_INF)
    m = jnp.max(s, axis=-1)
    p = jnp.exp(s - m[..., None])
    den = jnp.sum(p, axis=-1)
    num = jnp.einsum('bnhqk,bnkhd->bnqhd', p, vw)

    def from_sub(t):
        X = t.shape[-1]
        t = t.reshape(b, dil, Lp, H, X)[:, :, :L]
        return t.transpose(0, 2, 1, 3, 4).reshape(b, S, H, X)

    m_o = from_sub(m.transpose(0, 1, 3, 2)[..., None])[..., 0]
    den_o = from_sub(den.transpose(0, 1, 3, 2)[..., None])[..., 0]
    return m_o, den_o, from_sub(num)


def dilated_attention_mixer(h, w_qkv, w_out):
    b, S, _ = h.shape
    qkv = (h @ w_qkv).reshape(b, S, 3, ATT_HEADS, ATT_HEAD_DIM).astype(jnp.float32)
    q, k, v = qkv[:, :, 0], qkv[:, :, 1], qkv[:, :, 2]
    slopes = alibi_slopes(ATT_HEADS)
    parts = [dilated_branch(q, k, v, slopes, w, d) for (w, d) in DIL_PATTERNS]
    m_all = jnp.stack([pt[0] for pt in parts])
    den_all = jnp.stack([pt[1] for pt in parts])
    num_all = jnp.stack([pt[2] for pt in parts])
    wts = jnp.exp(m_all - jnp.max(m_all, axis=0, keepdims=True))
    den = jnp.sum(wts * den_all, axis=0)
    num = jnp.sum(wts[..., None] * num_all, axis=0)
    o = (num / den[..., None]).reshape(b, S, D_MODEL).astype(h.dtype)
    return o @ w_out


def ec_moe(h, w_router, w_gate, w_up, w_down):
    b, S, D = h.shape
    T = b * S
    cap = EC_CAPACITY_FACTOR * T // N_EXPERTS
    xt = h.reshape(T, D)
    aff = jax.nn.softmax((xt @ w_router).astype(jnp.float32), axis=-1)
    gates, idx = lax.top_k(aff.T, cap)

    def expert(args):
        idx_e, g_e, wg, wu, wd = args
        xe = xt[idx_e]
        ye = (jax.nn.silu(xe @ wg) * (xe @ wu)) @ wd
        return ye * g_e[:, None].astype(ye.dtype)

    ys = lax.map(expert, (idx, gates, w_gate, w_up, w_down))
    out = jnp.zeros((T, D), ys.dtype).at[idx.reshape(-1)].add(ys.reshape(-1, D))
    return out.reshape(b, S, D)


def trunk(x, norm_mix, norm_ffn, norm_final, ret_w_in, ret_w_out, ret_decay_fwd, ret_decay_bwd,
          att_w_qkv, att_w_out, moe_router, moe_w_gate, moe_w_up, moe_w_down):
    for i in range(DEPTH):
        hn = rmsnorm(x, norm_mix[i])
        j = i // N_MIXERS
        if i % N_MIXERS == 0:
            x = x + retention_mixer(hn, ret_w_in[j], ret_w_out[j], ret_decay_fwd[j], ret_decay_bwd[j])
        else:
            x = x + dilated_attention_mixer(hn, att_w_qkv[j], att_w_out[j])
        hn = rmsnorm(x, norm_ffn[i])
        x = x + ec_moe(hn, moe_router[i], moe_w_gate[i], moe_w_up[i], moe_w_down[i])
    return rmsnorm(x, norm_final)


def setup_inputs(seed: int = 0) -> dict:
    key = jax.random.key(seed)
    ks = jax.random.split(key, 18)
    f32 = jnp.float32
    nrm = lambda k, shape, scale: jax.random.normal(k, shape, f32) * scale
    base_decay = jnp.asarray(np.log(-np.log(1.0 - 2.0 ** (-5.0 - np.arange(RET_HEADS)))), f32)
    return {
        "x_prompt": nrm(ks[0], (BATCH, SEQ, D_MODEL), 1.0),
        "x_sample": nrm(ks[1], (DEC_BATCH, DEC_SEQ, D_MODEL), 1.0),
        "norm_mix": 1.0 + nrm(ks[2], (DEPTH, D_MODEL), 0.02),
        "norm_ffn": 1.0 + nrm(ks[3], (DEPTH, D_MODEL), 0.02),
        "norm_final": 1.0 + nrm(ks[4], (D_MODEL,), 0.02),
        "ret_w_in": nrm(ks[5], (N_RET_LAYERS, D_MODEL, RET_IN), D_MODEL ** -0.5),
        "ret_w_out": nrm(ks[6], (N_RET_LAYERS, RET_V_WIDTH, D_MODEL), RET_V_WIDTH ** -0.5),
        "ret_decay_fwd": base_decay[None] + nrm(ks[7], (N_RET_LAYERS, RET_HEADS), 0.1),
        "ret_decay_bwd": base_decay[None] + nrm(ks[8], (N_RET_LAYERS, RET_HEADS), 0.1),
        "att_w_qkv": nrm(ks[9], (N_ATT_LAYERS, D_MODEL, 3 * D_MODEL), D_MODEL ** -0.5),
        "att_w_out": nrm(ks[10], (N_ATT_LAYERS, D_MODEL, D_MODEL), D_MODEL ** -0.5),
        "moe_router": nrm(ks[11], (DEPTH, D_MODEL, N_EXPERTS), D_MODEL ** -0.5),
        "moe_w_gate": nrm(ks[12], (DEPTH, N_EXPERTS, D_MODEL, D_EXPERT), D_MODEL ** -0.5),
        "moe_w_up": nrm(ks[13], (DEPTH, N_EXPERTS, D_MODEL, D_EXPERT), D_MODEL ** -0.5),
        "moe_w_down": nrm(ks[14], (DEPTH, N_EXPERTS, D_EXPERT, D_MODEL), D_EXPERT ** -0.5),
    }


def reference(x_prompt, x_sample, norm_mix, norm_ffn, norm_final, ret_w_in, ret_w_out,
              ret_decay_fwd, ret_decay_bwd, att_w_qkv, att_w_out, moe_router, moe_w_gate,
              moe_w_up, moe_w_down):
    y_prompt = trunk(x_prompt, norm_mix, norm_ffn, norm_final, ret_w_in, ret_w_out, ret_decay_fwd,
                     ret_decay_bwd, att_w_qkv, att_w_out, moe_router, moe_w_gate, moe_w_up, moe_w_down)
    y_sample = trunk(x_sample, norm_mix, norm_ffn, norm_final, ret_w_in, ret_w_out, ret_decay_fwd,
                     ret_decay_bwd, att_w_qkv, att_w_out, moe_router, moe_w_gate, moe_w_up, moe_w_down)
    return (y_prompt, y_sample)
```

```python
import functools
import math

import jax
import jax.numpy as jnp
import numpy as np
from jax import lax
from jax.experimental import pallas as pl
from jax.experimental.pallas import tpu as pltpu

D_MODEL = 1024
RET_HEADS = 4
RET_QK_DIM = D_MODEL // RET_HEADS
RET_V_WIDTH = 2 * D_MODEL
RET_V_DIM = RET_V_WIDTH // RET_HEADS
ATT_HEADS = 16
ATT_HEAD_DIM = D_MODEL // ATT_HEADS
DIL_PATTERNS = ((128, 1), (512, 4), (2048, 16))
N_EXPERTS = 16
EC_CAPACITY_FACTOR = 2
RMS_EPS = 1e-6
NEG_INF = -1e30

LANES = 128
BF16_ROWS = 16
VMEM_LIMIT = 56 * 1024 * 1024

ROW_TILE = 512
MOE_TILE = 256
MOE_WIN = 64
MOE_MAX_PASSES = MOE_TILE // MOE_WIN + 1
FFN_ROWS = 512
FFN_CHUNK = 256


def _cparams(*sem):
    return pltpu.CompilerParams(dimension_semantics=sem, vmem_limit_bytes=VMEM_LIMIT)


def _norm_matmul_kernel(x_ref, g_ref, w_ref, o_ref, hn_ref):
    @pl.when(pl.program_id(1) == 0)
    def _():
        x = x_ref[...]
        ms = jnp.mean(x * x, axis=-1, keepdims=True)
        hn_ref[...] = (x * lax.rsqrt(ms + RMS_EPS) * g_ref[...]).astype(jnp.bfloat16)

    o_ref[...] = jnp.dot(hn_ref[...], w_ref[...],
                         preferred_element_type=jnp.float32).astype(o_ref.dtype)


def norm_matmul(x, g, w_bf16, out_dtype, tn):
    t, d = x.shape
    n = w_bf16.shape[1]
    return pl.pallas_call(
        _norm_matmul_kernel,
        grid=(t // ROW_TILE, n // tn),
        in_specs=[pl.BlockSpec((ROW_TILE, d), lambda i, j: (i, 0)),
                  pl.BlockSpec((1, d), lambda i, j: (0, 0)),
                  pl.BlockSpec((d, tn), lambda i, j: (0, j))],
        out_specs=pl.BlockSpec((ROW_TILE, tn), lambda i, j: (i, j)),
        out_shape=jax.ShapeDtypeStruct((t, n), out_dtype),
        scratch_shapes=[pltpu.VMEM((ROW_TILE, d), jnp.bfloat16)],
        compiler_params=_cparams("parallel", "arbitrary"),
        name="norm_matmul",
    )(x, g.reshape(1, d), w_bf16)


def _outproj_router_kernel(z_ref, w_ref, x_ref, g_ref, wr_ref, x1_ref, hn_ref, aff_ref):
    x1 = x_ref[...] + jnp.dot(z_ref[...], w_ref[...], preferred_element_type=jnp.float32)
    x1_ref[...] = x1
    ms = jnp.mean(x1 * x1, axis=-1, keepdims=True)
    hn = x1 * lax.rsqrt(ms + RMS_EPS) * g_ref[...]
    hn_ref[...] = hn.astype(jnp.bfloat16)
    logits = lax.dot_general(wr_ref[...], hn, (((1,), (1,)), ((), ())),
                             precision=lax.Precision.HIGHEST,
                             preferred_element_type=jnp.float32)
    m = jnp.max(logits, axis=0, keepdims=True)
    p = jnp.exp(logits - m)
    aff_ref[...] = p / jnp.sum(p, axis=0, keepdims=True)


def outproj_router(z, w_bf16, x, g, w_router):
    t, k = z.shape
    d = x.shape[1]
    e = w_router.shape[1]
    return pl.pallas_call(
        _outproj_router_kernel,
        grid=(t // ROW_TILE,),
        in_specs=[pl.BlockSpec((ROW_TILE, k), lambda i: (i, 0)),
                  pl.BlockSpec((k, d), lambda i: (0, 0)),
                  pl.BlockSpec((ROW_TILE, d), lambda i: (i, 0)),
                  pl.BlockSpec((1, d), lambda i: (0, 0)),
                  pl.BlockSpec((e, d), lambda i: (0, 0))],
        out_specs=[pl.BlockSpec((ROW_TILE, d), lambda i: (i, 0)),
                   pl.BlockSpec((ROW_TILE, d), lambda i: (i, 0)),
                   pl.BlockSpec((e, ROW_TILE), lambda i: (0, i))],
        out_shape=[jax.ShapeDtypeStruct((t, d), jnp.float32),
                   jax.ShapeDtypeStruct((t, d), jnp.bfloat16),
                   jax.ShapeDtypeStruct((e, t), jnp.float32)],
        compiler_params=_cparams("parallel"),
        name="outproj_router",
    )(z, w_bf16, x, g.reshape(1, d), w_router.T)


def _select_kernel(aff_ref, slot_ref, rowstart_ref, *, cap, base):
    e, r, _ = aff_ref.shape
    bits = pltpu.bitcast(aff_ref[...], jnp.int32)

    def count(mask):
        c = jnp.sum(jnp.where(mask, 1.0, 0.0), axis=2, keepdims=True)
        return jnp.sum(c, axis=1, keepdims=True)

    def search(i, thr):
        cand = thr | jnp.left_shift(jnp.int32(1), 30 - i)
        return jnp.where(count(bits >= cand) >= cap, cand, thr)

    thr = lax.fori_loop(0, 31, search, jnp.zeros((e, 1, 1), jnp.int32))
    gt = bits > thr
    eq = bits == thr
    need = cap - count(gt)

    row_i = lax.broadcasted_iota(jnp.int32, (LANES, LANES), 0)
    col_i = lax.broadcasted_iota(jnp.int32, (LANES, LANES), 1)
    upper = jnp.where(row_i <= col_i, 1.0, 0.0).astype(jnp.bfloat16)
    ones = jnp.ones((LANES, LANES), jnp.bfloat16)
    rr = lax.broadcasted_iota(jnp.int32, (r, r), 0)
    rc = lax.broadcasted_iota(jnp.int32, (r, r), 1)
    lower = jnp.where(rc < rr, 1.0, 0.0).astype(jnp.bfloat16)

    def excl_cumsum(mask):
        m = jnp.where(mask, 1.0, 0.0).astype(jnp.bfloat16).reshape(e * r, LANES)
        incl = jnp.dot(m, upper, preferred_element_type=jnp.float32)
        tot = jnp.dot(m, ones, preferred_element_type=jnp.float32)
        offs = []
        for ee in range(e):
            t_e = tot[ee * r:(ee + 1) * r].astype(jnp.bfloat16)
            offs.append(jnp.dot(lower, t_e, preferred_element_type=jnp.float32))
        off = jnp.concatenate(offs, axis=0)
        excl = incl - m.astype(jnp.float32) + off
        return excl.reshape(e, r, LANES), off.reshape(e, r, LANES)

    eq_rank, _ = excl_cumsum(eq)
    sel = gt | (eq & (eq_rank < need))
    pos, off = excl_cumsum(sel)
    slot_ref[...] = jnp.where(sel, pos.astype(jnp.int32) + base, -1)
    rowstart_ref[...] = off.astype(jnp.int32) + base


def select_tokens(aff3, cap, base):
    e, r, _ = aff3.shape
    return pl.pallas_call(
        functools.partial(_select_kernel, cap=cap, base=base),
        out_shape=[jax.ShapeDtypeStruct((e, r, LANES), jnp.int32),
                   jax.ShapeDtypeStruct((e, r, LANES), jnp.int32)],
        compiler_params=pltpu.CompilerParams(vmem_limit_bytes=VMEM_LIMIT),
        name="select_tokens",
    )(aff3)


def _dispatch_kernel(starts_ref, slot_ref, hn_ref, xg_ref, win_ref, carry_ref, sem_ref):
    i = pl.program_id(0)
    nt = pl.num_programs(0)
    e = slot_ref.shape[0]
    tt = slot_ref.shape[1]
    w = MOE_WIN
    buf = i % 2

    def aligned(s):
        return (s // BF16_ROWS) * BF16_ROWS

    a = [aligned(starts_ref[i * e + ee]) for ee in range(e)]
    end = [starts_ref[(i + 1) * e + ee] for ee in range(e)]
    n_pass = jnp.int32(1)
    for ee in range(e):
        n_pass = jnp.maximum(n_pass, (end[ee] - a[ee] + (w - 1)) // w)

    @pl.when(i == 0)
    def _():
        carry_ref[...] = jnp.zeros_like(carry_ref)
        win_ref[1, 0] = jnp.zeros(win_ref.shape[2:], win_ref.dtype)
        tail = [pltpu.make_async_copy(win_ref.at[1, 0], xg_ref.at[ee, pl.ds(r0, w), :], sem_ref.at[1, ee])
                for ee in range(e) for r0 in range(xg_ref.shape[1] - MOE_MAX_PASSES * w, xg_ref.shape[1], w)]
        for c in tail:
            c.start()
        for c in tail:
            c.wait()

    def copies(b, p):
        return [pltpu.make_async_copy(
            win_ref.at[b, ee],
            xg_ref.at[ee, pl.ds(pl.multiple_of(a[ee] + p * w, BF16_ROWS), w), :],
            sem_ref.at[b, ee]) for ee in range(e)]

    def wait_tile(b, src_i):
        for ee in range(e):
            pltpu.make_async_copy(win_ref.at[b, ee], xg_ref.at[ee, pl.ds(0, w), :],
                                  sem_ref.at[b, ee]).wait()

    hn = hn_ref[...]
    row = lax.broadcasted_iota(jnp.int32, (w, tt), 0)

    def one_pass(p, _):
        onehot = []
        for ee in range(e):
            rel = slot_ref[pl.ds(ee, 1), :] - (a[ee] + p * w)
            onehot.append(jnp.where(row == rel, 1.0, 0.0).astype(jnp.bfloat16))
        onehot = jnp.concatenate(onehot, axis=0)
        rows = jnp.dot(onehot, hn, preferred_element_type=jnp.float32)

        @pl.when(p > 0)
        def _():
            for c in copies(buf, p - 1):
                c.wait()

        for ee in range(e):
            r_e = rows[ee * w:(ee + 1) * w]
            head = r_e[:BF16_ROWS] + jnp.where(p == 0, carry_ref[ee].astype(jnp.float32), 0.0)
            win_ref[buf, ee, pl.ds(0, BF16_ROWS), :] = head.astype(jnp.bfloat16)
            win_ref[buf, ee, pl.ds(BF16_ROWS, w - BF16_ROWS), :] = r_e[BF16_ROWS:].astype(jnp.bfloat16)
            nxt = aligned(end[ee]) - (a[ee] + p * w)

            @pl.when((nxt >= 0) & (nxt < w))
            def _():
                carry_ref[ee] = win_ref[buf, ee, pl.ds(pl.multiple_of(nxt, BF16_ROWS), BF16_ROWS), :]

            @pl.when((p == n_pass - 1) & (nxt >= w))
            def _():
                carry_ref[ee] = jnp.zeros((BF16_ROWS, carry_ref.shape[2]), carry_ref.dtype)

        @pl.when((p == 0) & (i > 0))
        def _():
            wait_tile(1 - buf, i - 1)

        for c in copies(buf, p):
            c.start()
        return 0

    lax.fori_loop(0, n_pass, one_pass, 0)

    @pl.when(i == nt - 1)
    def _():
        wait_tile(buf, i)


def dispatch(starts, slot, hn, rows_padded):
    e, t = slot.shape
    d = hn.shape[1]
    grid_spec = pltpu.PrefetchScalarGridSpec(
        num_scalar_prefetch=1,
        grid=(t // MOE_TILE,),
        in_specs=[pl.BlockSpec((e, MOE_TILE), lambda i, s: (0, i)),
                  pl.BlockSpec((MOE_TILE, d), lambda i, s: (i, 0))],
        out_specs=pl.BlockSpec(memory_space=pl.ANY),
        scratch_shapes=[pltpu.VMEM((2, e, MOE_WIN, d), jnp.bfloat16),
                        pltpu.VMEM((e, BF16_ROWS, d), jnp.bfloat16),
                        pltpu.SemaphoreType.DMA((2, e))])
    return pl.pallas_call(
        _dispatch_kernel,
        grid_spec=grid_spec,
        out_shape=jax.ShapeDtypeStruct((e, rows_padded, d), jnp.bfloat16),
        compiler_params=_cparams("arbitrary"),
        name="moe_dispatch",
    )(starts, slot, hn)


def _ffn_kernel(x_ref, wg_ref, wu_ref, wd_ref, y_ref, acc_ref):
    x = x_ref[...]
    f = wg_ref.shape[1]
    for c in range(f // FFN_CHUNK):
        cols = slice(c * FFN_CHUNK, (c + 1) * FFN_CHUNK)
        g = jnp.dot(x, wg_ref[:, cols], preferred_element_type=jnp.float32)
        u = jnp.dot(x, wu_ref[:, cols], preferred_element_type=jnp.float32)
        h = (g * jax.nn.sigmoid(g) * u).astype(jnp.bfloat16)
        part = jnp.dot(h, wd_ref[cols, :], preferred_element_type=jnp.float32)
        if c == 0:
            acc_ref[...] = part
        else:
            acc_ref[...] += part
    y_ref[...] = acc_ref[...].astype(y_ref.dtype)


def expert_ffn(xg, wg, wu, wd, rows):
    e, _, d = xg.shape
    f = wg.shape[2]
    return pl.pallas_call(
        _ffn_kernel,
        grid=(e, rows // FFN_ROWS),
        in_specs=[pl.BlockSpec((None, FFN_ROWS, d), lambda ee, m: (ee, m, 0)),
                  pl.BlockSpec((None, d, f), lambda ee, m: (ee, 0, 0)),
                  pl.BlockSpec((None, d, f), lambda ee, m: (ee, 0, 0)),
                  pl.BlockSpec((None, f, d), lambda ee, m: (ee, 0, 0))],
        out_specs=pl.BlockSpec((None, FFN_ROWS, d), lambda ee, m: (ee, m, 0)),
        out_shape=jax.ShapeDtypeStruct((e, rows, d), jnp.bfloat16),
        scratch_shapes=[pltpu.VMEM((FFN_ROWS, d), jnp.float32)],
        compiler_params=_cparams("parallel", "arbitrary"),
        name="expert_ffn",
    )(xg, wg, wu, wd)


def _combine_kernel(starts_ref, slot_ref, gate_ref, x_ref, gain_ref, ys_ref, o_ref, buf_ref, sem_ref,
                    *, rows, final):
    i = pl.program_id(0)
    nt = pl.num_programs(0)
    tt, e = slot_ref.shape
    w = MOE_WIN
    d = x_ref.shape[1]
    b = i % 2

    def aligned(s):
        return (s // BF16_ROWS) * BF16_ROWS

    def window_start(ti, ee, p):
        return jnp.minimum(aligned(starts_ref[ti * e + ee]) + p * w, rows - w)

    def copies(ti, bb, p):
        return [pltpu.make_async_copy(
            ys_ref.at[ee, pl.ds(pl.multiple_of(window_start(ti, ee, p), BF16_ROWS), w), :],
            buf_ref.at[bb, pl.ds(ee * w, w), :],
            sem_ref.at[bb, ee]) for ee in range(e)]

    @pl.when(i == 0)
    def _():
        for c in copies(i, b, 0):
            c.start()

    @pl.when(i + 1 < nt)
    def _():
        for c in copies(i + 1, 1 - b, 0):
            c.start()

    n_pass = jnp.int32(1)
    for ee in range(e):
        n_pass = jnp.maximum(
            n_pass, (starts_ref[(i + 1) * e + ee] - aligned(starts_ref[i * e + ee]) + (w - 1)) // w)

    lane = lax.broadcasted_iota(jnp.int32, (tt, LANES), 1)
    slot = slot_ref[...]
    gate = gate_ref[...]
    per_tile = LANES // w

    def one_pass(p, acc):
        @pl.when(p > 0)
        def _():
            for c in copies(i, b, p):
                c.start()

        for c in copies(i, b, p):
            c.wait()

        q = []
        for grp in range(e // per_tile):
            tgt = jnp.full((tt, LANES), -1, jnp.int32)
            gv = jnp.zeros((tt, LANES), jnp.float32)
            for k in range(per_tile):
                ee = grp * per_tile + k
                lo = aligned(starts_ref[i * e + ee]) + p * w
                s_e = slot[:, ee:ee + 1]
                ok = (s_e >= lo) & (s_e < lo + w)
                t_e = jnp.where(ok, s_e - window_start(i, ee, p) + k * w, -1)
                in_grp = (lane >= k * w) & (lane < (k + 1) * w)
                tgt = jnp.where(in_grp, t_e, tgt)
                gv = jnp.where(in_grp, gate[:, ee:ee + 1], gv)
            q.append(jnp.where(lane == tgt, gv, 0.0))
        q = jnp.concatenate(q, axis=1)
        q_hi = q.astype(jnp.bfloat16)
        q_lo = (q - q_hi.astype(jnp.float32)).astype(jnp.bfloat16)
        ys = buf_ref[b]
        return (acc + jnp.dot(q_hi, ys, preferred_element_type=jnp.float32)
                + jnp.dot(q_lo, ys, preferred_element_type=jnp.float32))

    moe = lax.fori_loop(0, n_pass, one_pass, jnp.zeros((tt, d), jnp.float32))
    x = x_ref[...] + moe
    if final:
        ms = jnp.mean(x * x, axis=-1, keepdims=True)
        x = x * lax.rsqrt(ms + RMS_EPS) * gain_ref[...]
    o_ref[...] = x


def combine(starts, slot_t, gate_t, x, gain, ys, final):
    t, e = slot_t.shape
    d = x.shape[1]
    rows = ys.shape[1]
    grid_spec = pltpu.PrefetchScalarGridSpec(
        num_scalar_prefetch=1,
        grid=(t // MOE_TILE,),
        in_specs=[pl.BlockSpec((MOE_TILE, e), lambda i, s: (i, 0)),
                  pl.BlockSpec((MOE_TILE, e), lambda i, s: (i, 0)),
                  pl.BlockSpec((MOE_TILE, d), lambda i, s: (i, 0)),
                  pl.BlockSpec((1, d), lambda i, s: (0, 0)),
                  pl.BlockSpec(memory_space=pl.ANY)],
        out_specs=pl.BlockSpec((MOE_TILE, d), lambda i, s: (i, 0)),
        scratch_shapes=[pltpu.VMEM((2, e * MOE_WIN, d), jnp.bfloat16),
                        pltpu.SemaphoreType.DMA((2, e))])
    return pl.pallas_call(
        functools.partial(_combine_kernel, rows=rows, final=final),
        grid_spec=grid_spec,
        out_shape=jax.ShapeDtypeStruct((t, d), jnp.float32),
        compiler_params=_cparams("arbitrary"),
        name="moe_combine",
    )(starts, slot_t, gate_t, x, gain.reshape(1, d), ys)


def moe_layer(x1, hn, aff, groups, wg, wu, wd, gain, final):
    e, t = aff.shape
    slots, rowstarts = [], []
    base = 0
    for (t0, tg) in groups:
        cap = EC_CAPACITY_FACTOR * tg // e
        aff3 = aff[:, t0:t0 + tg].reshape(e, tg // LANES, LANES)
        s, r = select_tokens(aff3, cap, base)
        slots.append(s.reshape(e, tg))
        rowstarts.append(r[:, ::MOE_TILE // LANES, 0])
        base += cap
    rows = base
    slot = jnp.concatenate(slots, axis=1)
    starts = jnp.concatenate(rowstarts + [jnp.full((e, 1), rows, jnp.int32)], axis=1)
    starts = starts.T.reshape(-1)
    xg = dispatch(starts, slot, hn, rows + MOE_MAX_PASSES * MOE_WIN)
    ys = expert_ffn(xg, wg, wu, wd, rows)
    return combine(starts, slot.T, aff.T, x1, gain, ys, final)


RET_CHUNK = 256


def _retention_kernel(reset_ref, q_ref, k_ref, v_ref, intra_ref, qdec_ref, kdec_ref, cdec_ref,
                      *rest, reverse):
    if reverse:
        yf_ref, g_ref, o_ref, state_ref = rest
    else:
        o_ref, state_ref = rest
    i = pl.program_id(0)

    @pl.when(reset_ref[i] == 1)
    def _():
        state_ref[...] = jnp.zeros_like(state_ref)

    dk, dv = RET_QK_DIM, RET_V_DIM
    for h in range(RET_HEADS):
        q = q_ref[:, h * dk:(h + 1) * dk]
        k = k_ref[:, h * dk:(h + 1) * dk]
        v = v_ref[:, h * dv:(h + 1) * dv]
        state = state_ref[h]
        s = lax.dot_general(q, k, (((1,), (1,)), ((), ())), preferred_element_type=jnp.float32)
        inner = (s * intra_ref[h]).astype(jnp.bfloat16)
        qd = (q.astype(jnp.float32) * qdec_ref[h]).astype(jnp.bfloat16)
        y = (jnp.dot(inner, v, preferred_element_type=jnp.float32)
             + jnp.dot(qd, state.astype(jnp.bfloat16), preferred_element_type=jnp.float32))
        kd = (k.astype(jnp.float32) * kdec_ref[h]).astype(jnp.bfloat16)
        state_ref[h] = state * cdec_ref[h] + lax.dot_general(
            kd, v, (((0,), (0,)), ((), ())), preferred_element_type=jnp.float32)
        if reverse:
            y = y + yf_ref[:, h * dv:(h + 1) * dv].astype(jnp.float32)
            y = y * lax.rsqrt(jnp.mean(y * y, axis=-1, keepdims=True) + RMS_EPS)
            g = g_ref[:, h * dv:(h + 1) * dv].astype(jnp.float32)
            y = g * jax.nn.sigmoid(g) * y
        o_ref[:, h * dv:(h + 1) * dv] = y.astype(o_ref.dtype)


def _decay_tables(log_gamma, reverse):
    c = RET_CHUNK
    lg = log_gamma.astype(jnp.float32)[:, None, None]
    pos = jnp.arange(c, dtype=jnp.float32)
    rel = pos[:, None] - pos[None, :]
    scale = RET_QK_DIM ** -0.5
    if reverse:
        intra = jnp.where(rel < 0, jnp.exp(lg * jnp.maximum(-rel, 0.0)[None]), 0.0)
        qdec = jnp.exp(lg * (c - pos)[None, :, None])
        kdec = jnp.exp(lg * pos[None, :, None])
    else:
        intra = jnp.where(rel >= 0, jnp.exp(lg * jnp.maximum(rel, 0.0)[None]), 0.0)
        qdec = jnp.exp(lg * (pos + 1.0)[None, :, None])
        kdec = jnp.exp(lg * (c - 1.0 - pos)[None, :, None])
    return intra * scale, qdec, kdec * scale, jnp.exp(lg * c)


def retention(proj, seq_lens, decay, reverse, y_fwd=None):
    t = proj.shape[0]
    c = RET_CHUNK
    nc = t // c
    bounds = np.cumsum([0] + [s // c for s in seq_lens])
    reset = np.zeros((nc,), np.int32)
    if reverse:
        reset[nc - bounds[1:]] = 1
        chunk = lambda i, r: nc - 1 - i
    else:
        reset[bounds[:-1]] = 1
        chunk = lambda i, r: i
    intra, qdec, kdec, cdec = _decay_tables(-jnp.exp(decay.astype(jnp.float32)), reverse)
    d, vw, h = D_MODEL, RET_V_WIDTH, RET_HEADS
    const = lambda shape: pl.BlockSpec(shape, lambda i, r: (0,) * len(shape))
    in_specs = [pl.BlockSpec((c, d), lambda i, r: (chunk(i, r), 0)),
                pl.BlockSpec((c, d), lambda i, r: (chunk(i, r), 1)),
                pl.BlockSpec((c, vw), lambda i, r: (chunk(i, r), 1)),
                const((h, c, c)), const((h, c, 1)), const((h, c, 1)), const((h, 1, 1))]
    args = [jnp.asarray(reset), proj, proj, proj, intra, qdec, kdec, cdec]
    if reverse:
        in_specs += [pl.BlockSpec((c, vw), lambda i, r: (chunk(i, r), 0)),
                     pl.BlockSpec((c, vw), lambda i, r: (chunk(i, r), 2))]
        args += [y_fwd, proj]
    grid_spec = pltpu.PrefetchScalarGridSpec(
        num_scalar_prefetch=1, grid=(nc,), in_specs=in_specs,
        out_specs=pl.BlockSpec((c, vw), lambda i, r: (chunk(i, r), 0)),
        scratch_shapes=[pltpu.VMEM((h, RET_QK_DIM, RET_V_DIM), jnp.float32)])
    return pl.pallas_call(
        functools.partial(_retention_kernel, reverse=reverse),
        grid_spec=grid_spec,
        out_shape=jax.ShapeDtypeStruct((t, vw), jnp.bfloat16),
        compiler_params=_cparams("arbitrary"),
        name="retention_bwd" if reverse else "retention_fwd",
    )(*args)


ATT_BLOCK = 1024
ATT_HALF = 64
ATT_Q = 128


def _attention_kernel(vprev_ref, vnext_ref, q_ref, kp_ref, kc_ref, kn_ref, vp_ref, vc_ref, vn_ref,
                      bias_ref, o_ref, kwin_ref, vwin_ref, num_ref, m_ref, l_ref):
    i = pl.program_id(0)
    b = ATT_BLOCK
    half_lane = ATT_HEAD_DIM
    kwin_ref[0:b] = kp_ref[...]
    kwin_ref[b:2 * b] = kc_ref[...]
    kwin_ref[2 * b:3 * b] = kn_ref[...]
    vwin_ref[0:b] = vp_ref[...]
    vwin_ref[b:2 * b] = vc_ref[...]
    vwin_ref[2 * b:3 * b] = vn_ref[...]
    has_prev = vprev_ref[i] == 1
    has_next = vnext_ref[i] == 1
    scale = ATT_HEAD_DIM ** -0.5

    for br, (_, d) in enumerate(DIL_PATTERNS):
        per_class = b // d
        nq = min(ATT_Q, per_class)
        nk = nq + 2 * ATT_HALF
        nsub = per_class // nq

        def unit(u, _, br=br, d=d, nq=nq, nk=nk, nsub=nsub):
            r = u // nsub
            j = u % nsub
            q0 = r + d * j * nq
            k0 = b + q0 - d * ATT_HALF
            if d == 1:
                q = q_ref[pl.ds(q0, nq), :]
                k = kwin_ref[pl.ds(k0, nk), :]
                v = vwin_ref[pl.ds(k0, nk), :]
            else:
                q = q_ref[pl.ds(q0, nq, stride=d), :]
                k = kwin_ref[pl.ds(k0, nk, stride=d), :]
                v = vwin_ref[pl.ds(k0, nk, stride=d), :]
            lane = lax.broadcasted_iota(jnp.int32, (nq, LANES), 1)
            first = lane < half_lane
            q = q * scale
            q2 = jnp.concatenate([jnp.where(first, q, 0.0), jnp.where(first, 0.0, q)], axis=0)
            s = lax.dot_general(q2.astype(jnp.bfloat16), k.astype(jnp.bfloat16),
                                (((1,), (1,)), ((), ())), preferred_element_type=jnp.float32)
            krow = k0 + d * lax.broadcasted_iota(jnp.int32, (1, nk), 1)
            outside = ((krow < b) & jnp.logical_not(has_prev)) | ((krow >= 2 * b) & jnp.logical_not(has_next))
            colmask = jnp.where(outside, NEG_INF, 0.0)
            bias = jnp.concatenate([bias_ref[br, 0, 0:nq, 0:nk], bias_ref[br, 1, 0:nq, 0:nk]], axis=0)
            s = s + bias + colmask
            m = jnp.max(s, axis=-1, keepdims=True)
            p = jnp.exp(s - m)
            l = jnp.sum(p, axis=-1, keepdims=True)
            pv = jnp.dot(p.astype(jnp.bfloat16), v.astype(jnp.bfloat16),
                         preferred_element_type=jnp.float32)
            num = jnp.where(first, pv[:nq], pv[nq:])
            mm = jnp.where(first, m[:nq], m[nq:])
            ll = jnp.where(first, l[:nq], l[nq:])
            if d == 1:
                rows = pl.ds(q0, nq)
            else:
                rows = pl.ds(q0, nq, stride=d)
            num_ref[br, rows, :] = num
            m_ref[br, rows, :] = mm
            l_ref[br, rows, :] = ll
            return 0

        lax.fori_loop(0, d * nsub, unit, 0)

    m_all = m_ref[...]
    m_max = jnp.max(m_all, axis=0)
    wts = jnp.exp(m_all - m_max[None])
    den = jnp.sum(wts * l_ref[...], axis=0)
    num = jnp.sum(wts * num_ref[...], axis=0)
    o_ref[...] = (num / den).astype(o_ref.dtype)


def _alibi_bias():
    slopes = jnp.exp2(-8.0 * jnp.arange(1, ATT_HEADS + 1, dtype=jnp.float32) / ATT_HEADS)
    qi = jnp.arange(ATT_Q)
    ki = jnp.arange(ATT_Q + 2 * ATT_HALF) - ATT_HALF
    rel = jnp.abs(ki[None, :] - qi[:, None])
    dil = jnp.asarray([d for _, d in DIL_PATTERNS], jnp.float32)
    bias = -slopes[:, None, None, None] * (dil[None, :, None, None] * rel.astype(jnp.float32)[None, None])
    bias = jnp.where((rel <= ATT_HALF)[None, None], bias, NEG_INF)
    return bias.reshape(ATT_HEADS // 2, 2, len(DIL_PATTERNS), *rel.shape).transpose(0, 2, 1, 3, 4)


def dilated_attention(qkv, seq_lens):
    t = qkv.shape[0]
    b = ATT_BLOCK
    nb = t // b
    bounds = np.cumsum([0] + [s // b for s in seq_lens])
    vprev = np.ones((nb,), np.int32)
    vnext = np.ones((nb,), np.int32)
    vprev[bounds[:-1]] = 0
    vnext[bounds[1:] - 1] = 0
    pairs = ATT_HEADS // 2
    blk = lambda which, off: pl.BlockSpec(
        (b, LANES), lambda i, hp, vp, vn: (jnp.clip(i + which, 0, nb - 1), off + hp))
    bias = _alibi_bias()
    grid_spec = pltpu.PrefetchScalarGridSpec(
        num_scalar_prefetch=2, grid=(nb, pairs),
        in_specs=[blk(0, 0),
                  blk(-1, pairs), blk(0, pairs), blk(1, pairs),
                  blk(-1, 2 * pairs), blk(0, 2 * pairs), blk(1, 2 * pairs),
                  pl.BlockSpec((None,) + bias.shape[1:], lambda i, hp, vp, vn: (hp, 0, 0, 0, 0))],
        out_specs=pl.BlockSpec((b, LANES), lambda i, hp, vp, vn: (i, hp)),
        scratch_shapes=[pltpu.VMEM((3 * b, LANES), jnp.float32),
                        pltpu.VMEM((3 * b, LANES), jnp.float32),
                        pltpu.VMEM((len(DIL_PATTERNS), b, LANES), jnp.float32),
                        pltpu.VMEM((len(DIL_PATTERNS), b, LANES), jnp.float32),
                        pltpu.VMEM((len(DIL_PATTERNS), b, LANES), jnp.float32)])
    return pl.pallas_call(
        _attention_kernel,
        grid_spec=grid_spec,
        out_shape=jax.ShapeDtypeStruct((t, D_MODEL), jnp.bfloat16),
        compiler_params=_cparams("parallel", "arbitrary"),
        name="dilated_attention",
    )(jnp.asarray(vprev), jnp.asarray(vnext), qkv, qkv, qkv, qkv, qkv, qkv, qkv, bias)


PROJ_COLS = 1024


def kernel(x_prompt, x_sample, norm_mix, norm_ffn, norm_final, ret_w_in, ret_w_out, ret_decay_fwd,
           ret_decay_bwd, att_w_qkv, att_w_out, moe_router, moe_w_gate, moe_w_up, moe_w_down):
    d = x_prompt.shape[-1]
    depth = norm_mix.shape[0]
    tp = x_prompt.shape[0] * x_prompt.shape[1]
    ts = x_sample.shape[0] * x_sample.shape[1]
    seq_lens = [x_prompt.shape[1]] * x_prompt.shape[0] + [x_sample.shape[1]] * x_sample.shape[0]
    groups = ((0, tp), (tp, ts))
    bf16 = lambda w: w.astype(jnp.bfloat16)

    x = jnp.concatenate([x_prompt.reshape(tp, d), x_sample.reshape(ts, d)], axis=0)
    for i in range(depth):
        j = i // 2
        if i % 2 == 0:
            proj = norm_matmul(x, norm_mix[i], bf16(ret_w_in[j]), jnp.bfloat16, PROJ_COLS)
            y_fwd = retention(proj, seq_lens, ret_decay_fwd[j], False)
            z = retention(proj, seq_lens, ret_decay_bwd[j], True, y_fwd)
            w_out = ret_w_out[j]
        else:
            qkv = norm_matmul(x, norm_mix[i], bf16(att_w_qkv[j]), jnp.float32, PROJ_COLS)
            z = dilated_attention(qkv, seq_lens)
            w_out = att_w_out[j]
        x1, hn, aff = outproj_router(z, bf16(w_out), x, norm_ffn[i], moe_router[i])
        x = moe_layer(x1, hn, aff, groups, bf16(moe_w_gate[i]), bf16(moe_w_up[i]),
                      bf16(moe_w_down[i]), norm_final, final=(i == depth - 1))
    return x[:tp].reshape(x_prompt.shape), x[tp:].reshape(x_sample.shape)
```

```python
import functools
import math

import jax
import jax.numpy as jnp
import numpy as np
from jax import lax
from jax.experimental import pallas as pl
from jax.experimental.pallas import tpu as pltpu

D_MODEL = 1024
RET_HEADS = 4
RET_QK_DIM = D_MODEL // RET_HEADS
RET_V_WIDTH = 2 * D_MODEL
RET_V_DIM = RET_V_WIDTH // RET_HEADS
ATT_HEADS = 16
ATT_HEAD_DIM = D_MODEL // ATT_HEADS
DIL_PATTERNS = ((128, 1), (512, 4), (2048, 16))
N_EXPERTS = 16
EC_CAPACITY_FACTOR = 2
RMS_EPS = 1e-6
NEG_INF = -1e30

LANES = 128
BF16_ROWS = 16
VMEM_LIMIT = 56 * 1024 * 1024

ROW_TILE = 512
MOE_TILE = 256
MOE_WIN = 64
MOE_MAX_PASSES = MOE_TILE // MOE_WIN + 1
FFN_ROWS = 512
FFN_CHUNK = 256


def _cparams(*sem):
    return pltpu.CompilerParams(dimension_semantics=sem, vmem_limit_bytes=VMEM_LIMIT)


def _norm_matmul_kernel(x_ref, g_ref, w_ref, o_ref, *, tn):
    x = x_ref[...]
    ms = jnp.mean(x * x, axis=-1, keepdims=True)
    hn = (x * lax.rsqrt(ms + RMS_EPS) * g_ref[...]).astype(jnp.bfloat16)
    for c in range(w_ref.shape[1] // tn):
        cols = slice(c * tn, (c + 1) * tn)
        o_ref[:, cols] = jnp.dot(hn, w_ref[:, cols],
                                 preferred_element_type=jnp.float32).astype(o_ref.dtype)


def norm_matmul(x, g, w_bf16, out_dtype, tn):
    t, d = x.shape
    n = w_bf16.shape[1]
    return pl.pallas_call(
        functools.partial(_norm_matmul_kernel, tn=tn),
        grid=(t // ROW_TILE,),
        in_specs=[pl.BlockSpec((ROW_TILE, d), lambda i: (i, 0)),
                  pl.BlockSpec((1, d), lambda i: (0, 0)),
                  pl.BlockSpec((d, n), lambda i: (0, 0), pipeline_mode=pl.Buffered(1))],
        out_specs=pl.BlockSpec((ROW_TILE, n), lambda i: (i, 0)),
        out_shape=jax.ShapeDtypeStruct((t, n), out_dtype),
        compiler_params=_cparams("parallel"),
        name="norm_matmul",
    )(x, g.reshape(1, d), w_bf16)


def _outproj_router_kernel(z_ref, w_ref, x_ref, g_ref, wr_ref, x1_ref, hn_ref, aff_ref):
    x1 = x_ref[...] + jnp.dot(z_ref[...], w_ref[...], preferred_element_type=jnp.float32)
    x1_ref[...] = x1
    ms = jnp.mean(x1 * x1, axis=-1, keepdims=True)
    hn = x1 * lax.rsqrt(ms + RMS_EPS) * g_ref[...]
    hn_ref[...] = hn.astype(jnp.bfloat16)
    logits = lax.dot_general(wr_ref[...], hn, (((1,), (1,)), ((), ())),
                             precision=lax.Precision.HIGHEST,
                             preferred_element_type=jnp.float32)
    m = jnp.max(logits, axis=0, keepdims=True)
    p = jnp.exp(logits - m)
    aff_ref[...] = p / jnp.sum(p, axis=0, keepdims=True)


def outproj_router(z, w_bf16, x, g, w_router):
    t, k = z.shape
    d = x.shape[1]
    e = w_router.shape[1]
    return pl.pallas_call(
        _outproj_router_kernel,
        grid=(t // ROW_TILE,),
        in_specs=[pl.BlockSpec((ROW_TILE, k), lambda i: (i, 0)),
                  pl.BlockSpec((k, d), lambda i: (0, 0)),
                  pl.BlockSpec((ROW_TILE, d), lambda i: (i, 0)),
                  pl.BlockSpec((1, d), lambda i: (0, 0)),
                  pl.BlockSpec((e, d), lambda i: (0, 0))],
        out_specs=[pl.BlockSpec((ROW_TILE, d), lambda i: (i, 0)),
                   pl.BlockSpec((ROW_TILE, d), lambda i: (i, 0)),
                   pl.BlockSpec((e, ROW_TILE), lambda i: (0, i))],
        out_shape=[jax.ShapeDtypeStruct((t, d), jnp.float32),
                   jax.ShapeDtypeStruct((t, d), jnp.bfloat16),
                   jax.ShapeDtypeStruct((e, t), jnp.float32)],
        compiler_params=_cparams("parallel"),
        name="outproj_router",
    )(z, w_bf16, x, g.reshape(1, d), w_router.T)


def _select_kernel(aff_ref, slot_ref, rowstart_ref, *, cap, base):
    e, r, _ = aff_ref.shape
    bits = pltpu.bitcast(aff_ref[...], jnp.int32)

    def count(mask):
        c = jnp.sum(jnp.where(mask, 1.0, 0.0), axis=2, keepdims=True)
        return jnp.sum(c, axis=1, keepdims=True)

    def search(i, thr):
        cand = thr | jnp.left_shift(jnp.int32(1), 30 - i)
        return jnp.where(count(bits >= cand) >= cap, cand, thr)

    thr = lax.fori_loop(0, 31, search, jnp.zeros((e, 1, 1), jnp.int32))
    gt = bits > thr
    eq = bits == thr
    need = cap - count(gt)

    row_i = lax.broadcasted_iota(jnp.int32, (LANES, LANES), 0)
    col_i = lax.broadcasted_iota(jnp.int32, (LANES, LANES), 1)
    upper = jnp.where(row_i <= col_i, 1.0, 0.0).astype(jnp.bfloat16)
    ones = jnp.ones((LANES, LANES), jnp.bfloat16)
    rr = lax.broadcasted_iota(jnp.int32, (r, r), 0)
    rc = lax.broadcasted_iota(jnp.int32, (r, r), 1)
    lower = jnp.where(rc < rr, 1.0, 0.0).astype(jnp.bfloat16)

    def excl_cumsum(mask):
        m = jnp.where(mask, 1.0, 0.0).astype(jnp.bfloat16).reshape(e * r, LANES)
        incl = jnp.dot(m, upper, preferred_element_type=jnp.float32)
        tot = jnp.dot(m, ones, preferred_element_type=jnp.float32)
        offs = []
        for ee in range(e):
            t_e = tot[ee * r:(ee + 1) * r].astype(jnp.bfloat16)
            offs.append(jnp.dot(lower, t_e, preferred_element_type=jnp.float32))
        off = jnp.concatenate(offs, axis=0)
        excl = incl - m.astype(jnp.float32) + off
        return excl.reshape(e, r, LANES), off.reshape(e, r, LANES)

    eq_rank, _ = excl_cumsum(eq)
    sel = gt | (eq & (eq_rank < need))
    pos, off = excl_cumsum(sel)
    slot_ref[...] = jnp.where(sel, pos.astype(jnp.int32) + base, -1)
    rowstart_ref[...] = off.astype(jnp.int32) + base


def select_tokens(aff3, cap, base):
    e, r, _ = aff3.shape
    return pl.pallas_call(
        functools.partial(_select_kernel, cap=cap, base=base),
        out_shape=[jax.ShapeDtypeStruct((e, r, LANES), jnp.int32),
                   jax.ShapeDtypeStruct((e, r, LANES), jnp.int32)],
        compiler_params=pltpu.CompilerParams(vmem_limit_bytes=VMEM_LIMIT),
        name="select_tokens",
    )(aff3)


def _dispatch_kernel(starts_ref, slot_ref, hn_ref, xg_ref, win_ref, carry_ref, sem_ref):
    i = pl.program_id(0)
    nt = pl.num_programs(0)
    e = slot_ref.shape[0]
    tt = slot_ref.shape[1]
    w = MOE_WIN
    buf = i % 2

    def aligned(s):
        return (s // BF16_ROWS) * BF16_ROWS

    a = [aligned(starts_ref[i * e + ee]) for ee in range(e)]
    end = [starts_ref[(i + 1) * e + ee] for ee in range(e)]
    n_pass = jnp.int32(1)
    for ee in range(e):
        n_pass = jnp.maximum(n_pass, (end[ee] - a[ee] + (w - 1)) // w)

    @pl.when(i == 0)
    def _():
        carry_ref[...] = jnp.zeros_like(carry_ref)
        win_ref[1, 0] = jnp.zeros(win_ref.shape[2:], win_ref.dtype)
        tail = [pltpu.make_async_copy(win_ref.at[1, 0], xg_ref.at[ee, pl.ds(r0, w), :], sem_ref.at[1, ee])
                for ee in range(e) for r0 in range(xg_ref.shape[1] - MOE_MAX_PASSES * w, xg_ref.shape[1], w)]
        for c in tail:
            c.start()
        for c in tail:
            c.wait()

    def copies(b, p):
        return [pltpu.make_async_copy(
            win_ref.at[b, ee],
            xg_ref.at[ee, pl.ds(pl.multiple_of(a[ee] + p * w, BF16_ROWS), w), :],
            sem_ref.at[b, ee]) for ee in range(e)]

    def wait_tile(b, src_i):
        for ee in range(e):
            pltpu.make_async_copy(win_ref.at[b, ee], xg_ref.at[ee, pl.ds(0, w), :],
                                  sem_ref.at[b, ee]).wait()

    hn = hn_ref[...]
    row = lax.broadcasted_iota(jnp.int32, (w, tt), 0)

    def one_pass(p, _):
        onehot = []
        for ee in range(e):
            rel = slot_ref[pl.ds(ee, 1), :] - (a[ee] + p * w)
            onehot.append(jnp.where(row == rel, 1.0, 0.0).astype(jnp.bfloat16))
        onehot = jnp.concatenate(onehot, axis=0)
        rows = jnp.dot(onehot, hn, preferred_element_type=jnp.float32)

        @pl.when(p > 0)
        def _():
            for c in copies(buf, p - 1):
                c.wait()

        for ee in range(e):
            r_e = rows[ee * w:(ee + 1) * w]
            head = r_e[:BF16_ROWS] + jnp.where(p == 0, carry_ref[ee].astype(jnp.float32), 0.0)
            win_ref[buf, ee, pl.ds(0, BF16_ROWS), :] = head.astype(jnp.bfloat16)
            win_ref[buf, ee, pl.ds(BF16_ROWS, w - BF16_ROWS), :] = r_e[BF16_ROWS:].astype(jnp.bfloat16)
            nxt = aligned(end[ee]) - (a[ee] + p * w)

            @pl.when((nxt >= 0) & (nxt < w))
            def _():
                carry_ref[ee] = win_ref[buf, ee, pl.ds(pl.multiple_of(nxt, BF16_ROWS), BF16_ROWS), :]

            @pl.when((p == n_pass - 1) & (nxt >= w))
            def _():
                carry_ref[ee] = jnp.zeros((BF16_ROWS, carry_ref.shape[2]), carry_ref.dtype)

        @pl.when((p == 0) & (i > 0))
        def _():
            wait_tile(1 - buf, i - 1)

        for c in copies(buf, p):
            c.start()
        return 0

    lax.fori_loop(0, n_pass, one_pass, 0)

    @pl.when(i == nt - 1)
    def _():
        wait_tile(buf, i)


def dispatch(starts, slot, hn, rows_padded):
    e, t = slot.shape
    d = hn.shape[1]
    grid_spec = pltpu.PrefetchScalarGridSpec(
        num_scalar_prefetch=1,
        grid=(t // MOE_TILE,),
        in_specs=[pl.BlockSpec((e, MOE_TILE), lambda i, s: (0, i)),
                  pl.BlockSpec((MOE_TILE, d), lambda i, s: (i, 0))],
        out_specs=pl.BlockSpec(memory_space=pl.ANY),
        scratch_shapes=[pltpu.VMEM((2, e, MOE_WIN, d), jnp.bfloat16),
                        pltpu.VMEM((e, BF16_ROWS, d), jnp.bfloat16),
                        pltpu.SemaphoreType.DMA((2, e))])
    return pl.pallas_call(
        _dispatch_kernel,
        grid_spec=grid_spec,
        out_shape=jax.ShapeDtypeStruct((e, rows_padded, d), jnp.bfloat16),
        compiler_params=_cparams("arbitrary"),
        name="moe_dispatch",
    )(starts, slot, hn)


def _ffn_kernel(x_ref, wg_ref, wu_ref, wd_ref, y_ref, acc_ref):
    x = x_ref[...]
    f = wg_ref.shape[1]
    for c in range(f // FFN_CHUNK):
        cols = slice(c * FFN_CHUNK, (c + 1) * FFN_CHUNK)
        g = jnp.dot(x, wg_ref[:, cols], preferred_element_type=jnp.float32)
        u = jnp.dot(x, wu_ref[:, cols], preferred_element_type=jnp.float32)
        h = (g * jax.nn.sigmoid(g) * u).astype(jnp.bfloat16)
        part = jnp.dot(h, wd_ref[cols, :], preferred_element_type=jnp.float32)
        if c == 0:
            acc_ref[...] = part
        else:
            acc_ref[...] += part
    y_ref[...] = acc_ref[...].astype(y_ref.dtype)


def expert_ffn(xg, wg, wu, wd, rows):
    e, _, d = xg.shape
    f = wg.shape[2]
    return pl.pallas_call(
        _ffn_kernel,
        grid=(e, rows // FFN_ROWS),
        in_specs=[pl.BlockSpec((None, FFN_ROWS, d), lambda ee, m: (ee, m, 0)),
                  pl.BlockSpec((None, d, f), lambda ee, m: (ee, 0, 0)),
                  pl.BlockSpec((None, d, f), lambda ee, m: (ee, 0, 0)),
                  pl.BlockSpec((None, f, d), lambda ee, m: (ee, 0, 0))],
        out_specs=pl.BlockSpec((None, FFN_ROWS, d), lambda ee, m: (ee, m, 0)),
        out_shape=jax.ShapeDtypeStruct((e, rows, d), jnp.bfloat16),
        scratch_shapes=[pltpu.VMEM((FFN_ROWS, d), jnp.float32)],
        compiler_params=_cparams("parallel", "arbitrary"),
        name="expert_ffn",
    )(xg, wg, wu, wd)


def _combine_kernel(starts_ref, slot_ref, gate_ref, x_ref, gain_ref, ys_ref, o_ref, buf_ref, sem_ref,
                    *, rows, final):
    i = pl.program_id(0)
    nt = pl.num_programs(0)
    tt, e = slot_ref.shape
    w = MOE_WIN
    d = x_ref.shape[1]
    b = i % 2

    def aligned(s):
        return (s // BF16_ROWS) * BF16_ROWS

    def window_start(ti, ee, p):
        return jnp.minimum(aligned(starts_ref[ti * e + ee]) + p * w, rows - w)

    def copies(ti, bb, p):
        return [pltpu.make_async_copy(
            ys_ref.at[ee, pl.ds(pl.multiple_of(window_start(ti, ee, p), BF16_ROWS), w), :],
            buf_ref.at[bb, pl.ds(ee * w, w), :],
            sem_ref.at[bb, ee]) for ee in range(e)]

    @pl.when(i == 0)
    def _():
        for c in copies(i, b, 0):
            c.start()

    @pl.when(i + 1 < nt)
    def _():
        for c in copies(i + 1, 1 - b, 0):
            c.start()

    n_pass = jnp.int32(1)
    for ee in range(e):
        n_pass = jnp.maximum(
            n_pass, (starts_ref[(i + 1) * e + ee] - aligned(starts_ref[i * e + ee]) + (w - 1)) // w)

    lane = lax.broadcasted_iota(jnp.int32, (tt, LANES), 1)
    slot = slot_ref[...]
    gate = gate_ref[...]
    per_tile = LANES // w

    def one_pass(p, acc):
        @pl.when(p > 0)
        def _():
            for c in copies(i, b, p):
                c.start()

        for c in copies(i, b, p):
            c.wait()

        q = []
        for grp in range(e // per_tile):
            tgt = jnp.full((tt, LANES), -1, jnp.int32)
            gv = jnp.zeros((tt, LANES), jnp.float32)
            for k in range(per_tile):
                ee = grp * per_tile + k
                lo = aligned(starts_ref[i * e + ee]) + p * w
                s_e = slot[:, ee:ee + 1]
                ok = (s_e >= lo) & (s_e < lo + w)
                t_e = jnp.where(ok, s_e - window_start(i, ee, p) + k * w, -1)
                in_grp = (lane >= k * w) & (lane < (k + 1) * w)
                tgt = jnp.where(in_grp, t_e, tgt)
                gv = jnp.where(in_grp, gate[:, ee:ee + 1], gv)
            q.append(jnp.where(lane == tgt, gv, 0.0))
        q = jnp.concatenate(q, axis=1)
        q_hi = q.astype(jnp.bfloat16)
        q_lo = (q - q_hi.astype(jnp.float32)).astype(jnp.bfloat16)
        ys = buf_ref[b]
        return (acc + jnp.dot(q_hi, ys, preferred_element_type=jnp.float32)
                + jnp.dot(q_lo, ys, preferred_element_type=jnp.float32))

    moe = lax.fori_loop(0, n_pass, one_pass, jnp.zeros((tt, d), jnp.float32))
    x = x_ref[...] + moe
    if final:
        ms = jnp.mean(x * x, axis=-1, keepdims=True)
        x = x * lax.rsqrt(ms + RMS_EPS) * gain_ref[...]
    o_ref[...] = x


def combine(starts, slot_t, gate_t, x, gain, ys, final):
    t, e = slot_t.shape
    d = x.shape[1]
    rows = ys.shape[1]
    grid_spec = pltpu.PrefetchScalarGridSpec(
        num_scalar_prefetch=1,
        grid=(t // MOE_TILE,),
        in_specs=[pl.BlockSpec((MOE_TILE, e), lambda i, s: (i, 0)),
                  pl.BlockSpec((MOE_TILE, e), lambda i, s: (i, 0)),
                  pl.BlockSpec((MOE_TILE, d), lambda i, s: (i, 0)),
                  pl.BlockSpec((1, d), lambda i, s: (0, 0)),
                  pl.BlockSpec(memory_space=pl.ANY)],
        out_specs=pl.BlockSpec((MOE_TILE, d), lambda i, s: (i, 0)),
        scratch_shapes=[pltpu.VMEM((2, e * MOE_WIN, d), jnp.bfloat16),
                        pltpu.SemaphoreType.DMA((2, e))])
    return pl.pallas_call(
        functools.partial(_combine_kernel, rows=rows, final=final),
        grid_spec=grid_spec,
        out_shape=jax.ShapeDtypeStruct((t, d), jnp.float32),
        compiler_params=_cparams("arbitrary"),
        name="moe_combine",
    )(starts, slot_t, gate_t, x, gain.reshape(1, d), ys)


def moe_layer(x1, hn, aff, groups, wg, wu, wd, gain, final):
    e, t = aff.shape
    slots, rowstarts = [], []
    base = 0
    for (t0, tg) in groups:
        cap = EC_CAPACITY_FACTOR * tg // e
        aff3 = aff[:, t0:t0 + tg].reshape(e, tg // LANES, LANES)
        s, r = select_tokens(aff3, cap, base)
        slots.append(s.reshape(e, tg))
        rowstarts.append(r[:, ::MOE_TILE // LANES, 0])
        base += cap
    rows = base
    slot = jnp.concatenate(slots, axis=1)
    starts = jnp.concatenate(rowstarts + [jnp.full((e, 1), rows, jnp.int32)], axis=1)
    starts = starts.T.reshape(-1)
    xg = dispatch(starts, slot, hn, rows + MOE_MAX_PASSES * MOE_WIN)
    ys = expert_ffn(xg, wg, wu, wd, rows)
    return combine(starts, slot.T, aff.T, x1, gain, ys, final)


RET_CHUNK = 256


def _retention_kernel(reset_ref, q_ref, k_ref, v_ref, intra_ref, qdec_ref, kdec_ref, cdec_ref,
                      *rest, reverse):
    if reverse:
        yf_ref, g_ref, o_ref, state_ref = rest
    else:
        o_ref, state_ref = rest
    i = pl.program_id(0)

    @pl.when(reset_ref[i] == 1)
    def _():
        state_ref[...] = jnp.zeros_like(state_ref)

    dk, dv = RET_QK_DIM, RET_V_DIM
    for h in range(RET_HEADS):
        q = q_ref[:, h * dk:(h + 1) * dk]
        k = k_ref[:, h * dk:(h + 1) * dk]
        v = v_ref[:, h * dv:(h + 1) * dv]
        state = state_ref[h]
        s = lax.dot_general(q, k, (((1,), (1,)), ((), ())), preferred_element_type=jnp.float32)
        inner = (s * intra_ref[h]).astype(jnp.bfloat16)
        qd = (q.astype(jnp.float32) * qdec_ref[h]).astype(jnp.bfloat16)
        y = (jnp.dot(inner, v, preferred_element_type=jnp.float32)
             + jnp.dot(qd, state.astype(jnp.bfloat16), preferred_element_type=jnp.float32))
        kd = (k.astype(jnp.float32) * kdec_ref[h]).astype(jnp.bfloat16)
        state_ref[h] = state * cdec_ref[h] + lax.dot_general(
            kd, v, (((0,), (0,)), ((), ())), preferred_element_type=jnp.float32)
        if reverse:
            y = y + yf_ref[:, h * dv:(h + 1) * dv].astype(jnp.float32)
            y = y * lax.rsqrt(jnp.mean(y * y, axis=-1, keepdims=True) + RMS_EPS)
            g = g_ref[:, h * dv:(h + 1) * dv].astype(jnp.float32)
            y = g * jax.nn.sigmoid(g) * y
        o_ref[:, h * dv:(h + 1) * dv] = y.astype(o_ref.dtype)


def _decay_tables(log_gamma, reverse):
    c = RET_CHUNK
    lg = log_gamma.astype(jnp.float32)[:, None, None]
    pos = jnp.arange(c, dtype=jnp.float32)
    rel = pos[:, None] - pos[None, :]
    scale = RET_QK_DIM ** -0.5
    if reverse:
        intra = jnp.where(rel < 0, jnp.exp(lg * jnp.maximum(-rel, 0.0)[None]), 0.0)
        qdec = jnp.exp(lg * (c - pos)[None, :, None])
        kdec = jnp.exp(lg * pos[None, :, None])
    else:
        intra = jnp.where(rel >= 0, jnp.exp(lg * jnp.maximum(rel, 0.0)[None]), 0.0)
        qdec = jnp.exp(lg * (pos + 1.0)[None, :, None])
        kdec = jnp.exp(lg * (c - 1.0 - pos)[None, :, None])
    return intra * scale, qdec, kdec * scale, jnp.exp(lg * c)


def retention(proj, seq_lens, decay, reverse, y_fwd=None):
    t = proj.shape[0]
    c = RET_CHUNK
    nc = t // c
    bounds = np.cumsum([0] + [s // c for s in seq_lens])
    reset = np.zeros((nc,), np.int32)
    if reverse:
        reset[nc - bounds[1:]] = 1
        chunk = lambda i, r: nc - 1 - i
    else:
        reset[bounds[:-1]] = 1
        chunk = lambda i, r: i
    intra, qdec, kdec, cdec = _decay_tables(-jnp.exp(decay.astype(jnp.float32)), reverse)
    d, vw, h = D_MODEL, RET_V_WIDTH, RET_HEADS
    const = lambda shape: pl.BlockSpec(shape, lambda i, r: (0,) * len(shape))
    in_specs = [pl.BlockSpec((c, d), lambda i, r: (chunk(i, r), 0)),
                pl.BlockSpec((c, d), lambda i, r: (chunk(i, r), 1)),
                pl.BlockSpec((c, vw), lambda i, r: (chunk(i, r), 1)),
                const((h, c, c)), const((h, c, 1)), const((h, c, 1)), const((h, 1, 1))]
    args = [jnp.asarray(reset), proj, proj, proj, intra, qdec, kdec, cdec]
    if reverse:
        in_specs += [pl.BlockSpec((c, vw), lambda i, r: (chunk(i, r), 0)),
                     pl.BlockSpec((c, vw), lambda i, r: (chunk(i, r), 2))]
        args += [y_fwd, proj]
    grid_spec = pltpu.PrefetchScalarGridSpec(
        num_scalar_prefetch=1, grid=(nc,), in_specs=in_specs,
        out_specs=pl.BlockSpec((c, vw), lambda i, r: (chunk(i, r), 0)),
        scratch_shapes=[pltpu.VMEM((h, RET_QK_DIM, RET_V_DIM), jnp.float32)])
    return pl.pallas_call(
        functools.partial(_retention_kernel, reverse=reverse),
        grid_spec=grid_spec,
        out_shape=jax.ShapeDtypeStruct((t, vw), jnp.bfloat16),
        compiler_params=_cparams("arbitrary"),
        name="retention_bwd" if reverse else "retention_fwd",
    )(*args)


ATT_BLOCK = 1024
ATT_HALF = 64
ATT_Q = 128
ATT_DEINT = 4
assert tuple(d for _, d in DIL_PATTERNS) == (1, ATT_DEINT, ATT_DEINT * ATT_DEINT)
assert all(w == 2 * ATT_HALF * d for w, d in DIL_PATTERNS)


def _attention_kernel(vprev_ref, vnext_ref, q_ref, kp_ref, kc_ref, kn_ref, vp_ref, vc_ref, vn_ref,
                      bias_ref, o_ref, q4_ref, k4_ref, v4_ref, kedge_ref, vedge_ref, edge_ref,
                      num1_ref, m1_ref, l1_ref, num4_ref, m4_ref, l4_ref, onat_ref,
                      q16_ref, k16_ref, v16_ref, num16_ref, m16_ref, l16_ref):
    i = pl.program_id(0)
    b = ATT_BLOCK
    g = ATT_DEINT
    bq = b // g
    h = ATT_HALF
    half_lane = ATT_HEAD_DIM
    no_prev = vprev_ref[i] == 0
    no_next = vnext_ref[i] == 0
    scale = ATT_HEAD_DIM ** -0.5 * math.log2(math.e)
    kv_blocks = ((kp_ref, vp_ref), (kc_ref, vc_ref), (kn_ref, vn_ref))

    for c in range(g):
        q4_ref[c] = q_ref[pl.ds(c, bq, stride=g), :]
        for blk, (k_blk, v_blk) in enumerate(kv_blocks):
            k4_ref[c, blk * bq:(blk + 1) * bq] = k_blk[pl.ds(c, bq, stride=g), :]
            v4_ref[c, blk * bq:(blk + 1) * bq] = v_blk[pl.ds(c, bq, stride=g), :]

    nk1 = ATT_Q + 2 * h
    for edge_buf, (prv, cur, nxt) in ((kedge_ref, (kp_ref, kc_ref, kn_ref)),
                                      (vedge_ref, (vp_ref, vc_ref, vn_ref))):
        edge_buf[0, 0:h] = prv[b - h:b]
        edge_buf[0, h:nk1] = cur[0:nk1 - h]
        edge_buf[1, 0:nk1 - h] = cur[b - (nk1 - h):b]
        edge_buf[1, nk1 - h:nk1] = nxt[0:h]

    for br, (_, d) in enumerate(DIL_PATTERNS):
        nq = min(ATT_Q, b // d)
        nk = nq + 2 * h
        col = lax.broadcasted_iota(jnp.int32, (1, nk), 1)
        before = jnp.where((col < h) & no_prev, NEG_INF, 0.0)
        after = jnp.where((col >= nk - h) & no_next, NEG_INF, 0.0)
        if b // d == nq:
            before = before + after
        for hd in range(2):
            edge_ref[br, 0, hd, 0:nq, 0:nk] = bias_ref[br, hd, 0:nq, 0:nk] + before
            edge_ref[br, 1, hd, 0:nq, 0:nk] = bias_ref[br, hd, 0:nq, 0:nk] + after

    def bias_of(br, j, nsub, nq, nk):
        if j == 0:
            return [edge_ref[br, 0, hd, 0:nq, 0:nk] for hd in range(2)]
        if j == nsub - 1:
            return [edge_ref[br, 1, hd, 0:nq, 0:nk] for hd in range(2)]
        return [bias_ref[br, hd, 0:nq, 0:nk] for hd in range(2)]

    def scores(load_q, load_k, bias):
        q = load_q()
        nq = q.shape[0]
        first = lax.broadcasted_iota(jnp.int32, (nq, LANES), 1) < half_lane
        q = q * scale
        q2 = jnp.concatenate([jnp.where(first, q, 0.0), jnp.where(first, 0.0, q)], axis=0)
        s = lax.dot_general(q2.astype(jnp.bfloat16), load_k().astype(jnp.bfloat16),
                            (((1,), (1,)), ((), ())), preferred_element_type=jnp.float32)
        return s + jnp.concatenate(bias(), axis=0)

    def softmax(s):
        m = jnp.max(s, axis=-1, keepdims=True)
        p = jnp.exp2(s - m)
        return p.astype(jnp.bfloat16), m, jnp.sum(p, axis=-1, keepdims=True)

    def values(p, m, l, load_v, store):
        nq = p.shape[0] // 2
        first = lax.broadcasted_iota(jnp.int32, (nq, LANES), 1) < half_lane
        pv = jnp.dot(p, load_v().astype(jnp.bfloat16), preferred_element_type=jnp.float32)
        store(jnp.where(first, pv[:nq], pv[nq:]), jnp.where(first, m[:nq], m[nq:]),
              jnp.where(first, l[:nq], l[nq:]))

    units = []

    def store_to(num_r, m_r, l_r, idx):
        def store(num, m, l):
            num_r[idx] = num
            m_r[idx] = m
            l_r[idx] = l
        return store

    nq, nk, nsub = ATT_Q, nk1, b // ATT_Q
    for j in range(nsub):
        rows = slice(j * nq, (j + 1) * nq)
        keys = slice(j * nq - h, j * nq - h + nk)
        if j == 0:
            load_k, load_v = (lambda: kedge_ref[0]), (lambda: vedge_ref[0])
        elif j == nsub - 1:
            load_k, load_v = (lambda: kedge_ref[1]), (lambda: vedge_ref[1])
        else:
            load_k, load_v = (lambda keys=keys: kc_ref[keys]), (lambda keys=keys: vc_ref[keys])
        units.append(((lambda rows=rows: q_ref[rows]), load_k, load_v,
                      (lambda j=j, a=(nsub, nq, nk): bias_of(0, j, *a)),
                      store_to(num1_ref, m1_ref, l1_ref, rows)))

    nq = min(ATT_Q, bq)
    nk, nsub = nq + 2 * h, bq // nq
    for c in range(g):
        for j in range(nsub):
            rows = slice(j * nq, (j + 1) * nq)
            keys = slice(bq + j * nq - h, bq + j * nq - h + nk)
            units.append(((lambda c=c, rows=rows: q4_ref[c, rows]),
                          (lambda c=c, keys=keys: k4_ref[c, keys]),
                          (lambda c=c, keys=keys: v4_ref[c, keys]),
                          (lambda j=j, a=(nsub, nq, nk): bias_of(1, j, *a)),
                          store_to(num4_ref, m4_ref, l4_ref, (0, c, rows))))

    nq = bq // g
    nk = nq + 2 * h
    for c in range(g):
        for a in range(g):
            q16_ref[c, a] = q4_ref[c, pl.ds(a, nq, stride=g)]
            k16_ref[c, a] = k4_ref[c, pl.ds(a, nk, stride=g)]
            v16_ref[c, a] = v4_ref[c, pl.ds(a, nk, stride=g)]
            units.append(((lambda c=c, a=a: q16_ref[c, a]), (lambda c=c, a=a: k16_ref[c, a]),
                          (lambda c=c, a=a: v16_ref[c, a]),
                          (lambda a_=(1, nq, nk): bias_of(2, 0, *a_)),
                          store_to(num16_ref, m16_ref, l16_ref, (c, a))))

    s_prev = None
    sm_prev = None
    for t in range(len(units) + 2):
        s_new = scores(units[t][0], units[t][1], units[t][3]) if t < len(units) else None
        sm_new = softmax(s_prev) if s_prev is not None else None
        if sm_prev is not None:
            values(*sm_prev, units[t - 2][2], units[t - 2][4])
        s_prev, sm_prev = s_new, sm_new

    for c in range(g):
        for a in range(g):
            rows = pl.ds(a, nq, stride=g)
            num4_ref[1, c, rows] = num16_ref[c, a]
            m4_ref[1, c, rows] = m16_ref[c, a]
            l4_ref[1, c, rows] = l16_ref[c, a]

    for c in range(g):
        rows = pl.ds(c, bq, stride=g)
        ms = [m1_ref[rows], m4_ref[0, c], m4_ref[1, c]]
        ls = [l1_ref[rows], l4_ref[0, c], l4_ref[1, c]]
        nums = [num1_ref[rows], num4_ref[0, c], num4_ref[1, c]]
        m_max = jnp.maximum(jnp.maximum(ms[0], ms[1]), ms[2])
        wts = [jnp.exp2(mm - m_max) for mm in ms]
        den = wts[0] * ls[0] + wts[1] * ls[1] + wts[2] * ls[2]
        num = wts[0] * nums[0] + wts[1] * nums[1] + wts[2] * nums[2]
        onat_ref[rows] = num / den
    o_ref[...] = onat_ref[...].astype(o_ref.dtype)


def _alibi_bias():
    slopes = jnp.exp2(-8.0 * jnp.arange(1, ATT_HEADS + 1, dtype=jnp.float32) / ATT_HEADS)
    qi = jnp.arange(ATT_Q)
    ki = jnp.arange(ATT_Q + 2 * ATT_HALF) - ATT_HALF
    rel = jnp.abs(ki[None, :] - qi[:, None])
    dil = jnp.asarray([d for _, d in DIL_PATTERNS], jnp.float32)
    bias = -slopes[:, None, None, None] * (dil[None, :, None, None] * rel.astype(jnp.float32)[None, None])
    bias = jnp.where((rel <= ATT_HALF)[None, None], bias * math.log2(math.e), NEG_INF)
    return bias.reshape(ATT_HEADS // 2, 2, len(DIL_PATTERNS), *rel.shape).transpose(0, 2, 1, 3, 4)


def dilated_attention(qkv, seq_lens):
    t = qkv.shape[0]
    b = ATT_BLOCK
    g = ATT_DEINT
    nb = t // b
    bounds = np.cumsum([0] + [s // b for s in seq_lens])
    vprev = np.ones((nb,), np.int32)
    vnext = np.ones((nb,), np.int32)
    vprev[bounds[:-1]] = 0
    vnext[bounds[1:] - 1] = 0
    pairs = ATT_HEADS // 2
    blk = lambda which, off: pl.BlockSpec(
        (b, LANES), lambda i, hp, vp, vn: (jnp.clip(i + which, 0, nb - 1), off + hp))
    bias = _alibi_bias()
    grid_spec = pltpu.PrefetchScalarGridSpec(
        num_scalar_prefetch=2, grid=(nb, pairs),
        in_specs=[blk(0, 0),
                  blk(-1, pairs), blk(0, pairs), blk(1, pairs),
                  blk(-1, 2 * pairs), blk(0, 2 * pairs), blk(1, 2 * pairs),
                  pl.BlockSpec((None,) + bias.shape[1:], lambda i, hp, vp, vn: (hp, 0, 0, 0, 0))],
        out_specs=pl.BlockSpec((b, LANES), lambda i, hp, vp, vn: (i, hp)),
        scratch_shapes=[pltpu.VMEM((g, b // g, LANES), jnp.float32),
                        pltpu.VMEM((g, 3 * b // g, LANES), jnp.float32),
                        pltpu.VMEM((g, 3 * b // g, LANES), jnp.float32),
                        pltpu.VMEM((2, ATT_Q + 2 * ATT_HALF, LANES), jnp.float32),
                        pltpu.VMEM((2, ATT_Q + 2 * ATT_HALF, LANES), jnp.float32),
                        pltpu.VMEM((len(DIL_PATTERNS), 2) + bias.shape[2:], jnp.float32)]
        + [pltpu.VMEM((b, LANES), jnp.float32)] * 3
        + [pltpu.VMEM((2, g, b // g, LANES), jnp.float32)] * 3
        + [pltpu.VMEM((b, LANES), jnp.float32)]
        + [pltpu.VMEM((g, g, b // (g * g), LANES), jnp.float32),
           pltpu.VMEM((g, g, b // (g * g) + 2 * ATT_HALF, LANES), jnp.float32),
           pltpu.VMEM((g, g, b // (g * g) + 2 * ATT_HALF, LANES), jnp.float32)]
        + [pltpu.VMEM((g, g, b // (g * g), LANES), jnp.float32)] * 3)
    return pl.pallas_call(
        _attention_kernel,
        grid_spec=grid_spec,
        out_shape=jax.ShapeDtypeStruct((t, D_MODEL), jnp.bfloat16),
        compiler_params=_cparams("parallel", "arbitrary"),
        name="dilated_attention",
    )(jnp.asarray(vprev), jnp.asarray(vnext), qkv, qkv, qkv, qkv, qkv, qkv, qkv, bias)


PROJ_COLS = 1024


def kernel(x_prompt, x_sample, norm_mix, norm_ffn, norm_final, ret_w_in, ret_w_out, ret_decay_fwd,
           ret_decay_bwd, att_w_qkv, att_w_out, moe_router, moe_w_gate, moe_w_up, moe_w_down):
    d = x_prompt.shape[-1]
    depth = norm_mix.shape[0]
    tp = x_prompt.shape[0] * x_prompt.shape[1]
    ts = x_sample.shape[0] * x_sample.shape[1]
    seq_lens = [x_prompt.shape[1]] * x_prompt.shape[0] + [x_sample.shape[1]] * x_sample.shape[0]
    groups = ((0, tp), (tp, ts))
    bf16 = lambda w: w.astype(jnp.bfloat16)

    x = jnp.concatenate([x_prompt.reshape(tp, d), x_sample.reshape(ts, d)], axis=0)
    for i in range(depth):
        j = i // 2
        if i % 2 == 0:
            proj = norm_matmul(x, norm_mix[i], bf16(ret_w_in[j]), jnp.bfloat16, PROJ_COLS)
            y_fwd = retention(proj, seq_lens, ret_decay_fwd[j], False)
            z = retention(proj, seq_lens, ret_decay_bwd[j], True, y_fwd)
            w_out = ret_w_out[j]
        else:
            qkv = norm_matmul(x, norm_mix[i], bf16(att_w_qkv[j]), jnp.float32, PROJ_COLS)
            z = dilated_attention(qkv, seq_lens)
            w_out = att_w_out[j]
        x1, hn, aff = outproj_router(z, bf16(w_out), x, norm_ffn[i], moe_router[i])
        x = moe_layer(x1, hn, aff, groups, bf16(moe_w_gate[i]), bf16(moe_w_up[i]),
                      bf16(moe_w_down[i]), norm_final, final=(i == depth - 1))
    return x[:tp].reshape(x_prompt.shape), x[tp:].reshape(x_sample.shape)
```

```python
import functools
import math

import jax
import jax.numpy as jnp
import numpy as np
from jax import lax
from jax.experimental import pallas as pl
from jax.experimental.pallas import tpu as pltpu

D_MODEL = 1024
RET_HEADS = 4
RET_QK_DIM = D_MODEL // RET_HEADS
RET_V_WIDTH = 2 * D_MODEL
RET_V_DIM = RET_V_WIDTH // RET_HEADS
ATT_HEADS = 16
ATT_HEAD_DIM = D_MODEL // ATT_HEADS
DIL_PATTERNS = ((128, 1), (512, 4), (2048, 16))
N_EXPERTS = 16
EC_CAPACITY_FACTOR = 2
RMS_EPS = 1e-6
NEG_INF = -1e30

LANES = 128
BF16_ROWS = 16
VMEM_LIMIT = 56 * 1024 * 1024

ROW_TILE = 512
MOE_TILE = 256
MOE_WIN = 64
MOE_MAX_PASSES = MOE_TILE // MOE_WIN + 1
FFN_ROWS = 512
FFN_CHUNK = 256


def _cparams(*sem):
    return pltpu.CompilerParams(dimension_semantics=sem, vmem_limit_bytes=VMEM_LIMIT)


def _norm_matmul_kernel(x_ref, g_ref, w_ref, o_ref, *, tn):
    x = x_ref[...]
    ms = jnp.mean(x * x, axis=-1, keepdims=True)
    hn = (x * lax.rsqrt(ms + RMS_EPS) * g_ref[...]).astype(jnp.bfloat16)
    for c in range(w_ref.shape[1] // tn):
        cols = slice(c * tn, (c + 1) * tn)
        o_ref[:, cols] = jnp.dot(hn, w_ref[:, cols],
                                 preferred_element_type=jnp.float32).astype(o_ref.dtype)


def norm_matmul(x, g, w_bf16, out_dtype, tn):
    t, d = x.shape
    n = w_bf16.shape[1]
    return pl.pallas_call(
        functools.partial(_norm_matmul_kernel, tn=tn),
        grid=(t // ROW_TILE,),
        in_specs=[pl.BlockSpec((ROW_TILE, d), lambda i: (i, 0)),
                  pl.BlockSpec((1, d), lambda i: (0, 0)),
                  pl.BlockSpec((d, n), lambda i: (0, 0), pipeline_mode=pl.Buffered(1))],
        out_specs=pl.BlockSpec((ROW_TILE, n), lambda i: (i, 0)),
        out_shape=jax.ShapeDtypeStruct((t, n), out_dtype),
        compiler_params=_cparams("parallel"),
        name="norm_matmul",
    )(x, g.reshape(1, d), w_bf16)


def _outproj_router_kernel(z_ref, w_ref, x_ref, g_ref, wr_ref, x1_ref, hn_ref, aff_ref):
    x1 = x_ref[...] + jnp.dot(z_ref[...], w_ref[...], preferred_element_type=jnp.float32)
    x1_ref[...] = x1
    ms = jnp.mean(x1 * x1, axis=-1, keepdims=True)
    hn = x1 * lax.rsqrt(ms + RMS_EPS) * g_ref[...]
    hn_ref[...] = hn.astype(jnp.bfloat16)
    logits = lax.dot_general(wr_ref[...], hn, (((1,), (1,)), ((), ())),
                             precision=lax.Precision.HIGHEST,
                             preferred_element_type=jnp.float32)
    m = jnp.max(logits, axis=0, keepdims=True)
    p = jnp.exp(logits - m)
    aff_ref[...] = p / jnp.sum(p, axis=0, keepdims=True)


def outproj_router(z, w_bf16, x, g, w_router):
    t, k = z.shape
    d = x.shape[1]
    e = w_router.shape[1]
    return pl.pallas_call(
        _outproj_router_kernel,
        grid=(t // ROW_TILE,),
        in_specs=[pl.BlockSpec((ROW_TILE, k), lambda i: (i, 0)),
                  pl.BlockSpec((k, d), lambda i: (0, 0)),
                  pl.BlockSpec((ROW_TILE, d), lambda i: (i, 0)),
                  pl.BlockSpec((1, d), lambda i: (0, 0)),
                  pl.BlockSpec((e, d), lambda i: (0, 0))],
        out_specs=[pl.BlockSpec((ROW_TILE, d), lambda i: (i, 0)),
                   pl.BlockSpec((ROW_TILE, d), lambda i: (i, 0)),
                   pl.BlockSpec((e, ROW_TILE), lambda i: (0, i))],
        out_shape=[jax.ShapeDtypeStruct((t, d), jnp.float32),
                   jax.ShapeDtypeStruct((t, d), jnp.bfloat16),
                   jax.ShapeDtypeStruct((e, t), jnp.float32)],
        compiler_params=_cparams("parallel"),
        name="outproj_router",
    )(z, w_bf16, x, g.reshape(1, d), w_router.T)


def _select_kernel(aff_ref, slot_ref, rowstart_ref, *, cap, base):
    e, r, _ = aff_ref.shape
    bits = pltpu.bitcast(aff_ref[...], jnp.int32)

    def count(mask):
        c = jnp.sum(jnp.where(mask, 1.0, 0.0), axis=2, keepdims=True)
        return jnp.sum(c, axis=1, keepdims=True)

    def search(i, thr):
        cand = thr | jnp.left_shift(jnp.int32(1), 30 - i)
        return jnp.where(count(bits >= cand) >= cap, cand, thr)

    thr = lax.fori_loop(0, 31, search, jnp.zeros((e, 1, 1), jnp.int32))
    gt = bits > thr
    eq = bits == thr
    need = cap - count(gt)

    row_i = lax.broadcasted_iota(jnp.int32, (LANES, LANES), 0)
    col_i = lax.broadcasted_iota(jnp.int32, (LANES, LANES), 1)
    upper = jnp.where(row_i <= col_i, 1.0, 0.0).astype(jnp.bfloat16)
    ones = jnp.ones((LANES, LANES), jnp.bfloat16)
    rr = lax.broadcasted_iota(jnp.int32, (r, r), 0)
    rc = lax.broadcasted_iota(jnp.int32, (r, r), 1)
    lower = jnp.where(rc < rr, 1.0, 0.0).astype(jnp.bfloat16)

    def excl_cumsum(mask):
        m = jnp.where(mask, 1.0, 0.0).astype(jnp.bfloat16).reshape(e * r, LANES)
        incl = jnp.dot(m, upper, preferred_element_type=jnp.float32)
        tot = jnp.dot(m, ones, preferred_element_type=jnp.float32)
        offs = []
        for ee in range(e):
            t_e = tot[ee * r:(ee + 1) * r].astype(jnp.bfloat16)
            offs.append(jnp.dot(lower, t_e, preferred_element_type=jnp.float32))
        off = jnp.concatenate(offs, axis=0)
        excl = incl - m.astype(jnp.float32) + off
        return excl.reshape(e, r, LANES), off.reshape(e, r, LANES)

    eq_rank, _ = excl_cumsum(eq)
    sel = gt | (eq & (eq_rank < need))
    pos, off = excl_cumsum(sel)
    slot_ref[...] = jnp.where(sel, pos.astype(jnp.int32) + base, -1)
    rowstart_ref[...] = off.astype(jnp.int32) + base


def select_tokens(aff3, cap, base):
    e, r, _ = aff3.shape
    return pl.pallas_call(
        functools.partial(_select_kernel, cap=cap, base=base),
        out_shape=[jax.ShapeDtypeStruct((e, r, LANES), jnp.int32),
                   jax.ShapeDtypeStruct((e, r, LANES), jnp.int32)],
        compiler_params=pltpu.CompilerParams(vmem_limit_bytes=VMEM_LIMIT),
        name="select_tokens",
    )(aff3)


def _dispatch_kernel(starts_ref, slot_ref, hn_ref, xg_ref, win_ref, carry_ref, sem_ref):
    i = pl.program_id(0)
    nt = pl.num_programs(0)
    e = slot_ref.shape[0]
    tt = slot_ref.shape[1]
    w = MOE_WIN
    buf = i % 2

    def aligned(s):
        return (s // BF16_ROWS) * BF16_ROWS

    a = [aligned(starts_ref[i * e + ee]) for ee in range(e)]
    end = [starts_ref[(i + 1) * e + ee] for ee in range(e)]
    n_pass = jnp.int32(1)
    for ee in range(e):
        n_pass = jnp.maximum(n_pass, (end[ee] - a[ee] + (w - 1)) // w)

    @pl.when(i == 0)
    def _():
        carry_ref[...] = jnp.zeros_like(carry_ref)
        win_ref[1, 0] = jnp.zeros(win_ref.shape[2:], win_ref.dtype)
        tail = [pltpu.make_async_copy(win_ref.at[1, 0], xg_ref.at[ee, pl.ds(r0, w), :], sem_ref.at[1, ee])
                for ee in range(e) for r0 in range(xg_ref.shape[1] - MOE_MAX_PASSES * w, xg_ref.shape[1], w)]
        for c in tail:
            c.start()
        for c in tail:
            c.wait()

    def copies(b, p):
        return [pltpu.make_async_copy(
            win_ref.at[b, ee],
            xg_ref.at[ee, pl.ds(pl.multiple_of(a[ee] + p * w, BF16_ROWS), w), :],
            sem_ref.at[b, ee]) for ee in range(e)]

    def wait_tile(b, src_i):
        for ee in range(e):
            pltpu.make_async_copy(win_ref.at[b, ee], xg_ref.at[ee, pl.ds(0, w), :],
                                  sem_ref.at[b, ee]).wait()

    hn = hn_ref[...]
    row = lax.broadcasted_iota(jnp.int32, (w, tt), 0)

    def one_pass(p, _):
        onehot = []
        for ee in range(e):
            rel = slot_ref[pl.ds(ee, 1), :] - (a[ee] + p * w)
            onehot.append(jnp.where(row == rel, 1.0, 0.0).astype(jnp.bfloat16))
        onehot = jnp.concatenate(onehot, axis=0)
        rows = jnp.dot(onehot, hn, preferred_element_type=jnp.float32)

        @pl.when(p > 0)
        def _():
            for c in copies(buf, p - 1):
                c.wait()

        for ee in range(e):
            r_e = rows[ee * w:(ee + 1) * w]
            head = r_e[:BF16_ROWS] + jnp.where(p == 0, carry_ref[ee].astype(jnp.float32), 0.0)
            win_ref[buf, ee, pl.ds(0, BF16_ROWS), :] = head.astype(jnp.bfloat16)
            win_ref[buf, ee, pl.ds(BF16_ROWS, w - BF16_ROWS), :] = r_e[BF16_ROWS:].astype(jnp.bfloat16)
            nxt = aligned(end[ee]) - (a[ee] + p * w)

            @pl.when((nxt >= 0) & (nxt < w))
            def _():
                carry_ref[ee] = win_ref[buf, ee, pl.ds(pl.multiple_of(nxt, BF16_ROWS), BF16_ROWS), :]

            @pl.when((p == n_pass - 1) & (nxt >= w))
            def _():
                carry_ref[ee] = jnp.zeros((BF16_ROWS, carry_ref.shape[2]), carry_ref.dtype)

        @pl.when((p == 0) & (i > 0))
        def _():
            wait_tile(1 - buf, i - 1)

        for c in copies(buf, p):
            c.start()
        return 0

    lax.fori_loop(0, n_pass, one_pass, 0)

    @pl.when(i == nt - 1)
    def _():
        wait_tile(buf, i)


def dispatch(starts, slot, hn, rows_padded):
    e, t = slot.shape
    d = hn.shape[1]
    grid_spec = pltpu.PrefetchScalarGridSpec(
        num_scalar_prefetch=1,
        grid=(t // MOE_TILE,),
        in_specs=[pl.BlockSpec((e, MOE_TILE), lambda i, s: (0, i)),
                  pl.BlockSpec((MOE_TILE, d), lambda i, s: (i, 0))],
        out_specs=pl.BlockSpec(memory_space=pl.ANY),
        scratch_shapes=[pltpu.VMEM((2, e, MOE_WIN, d), jnp.bfloat16),
                        pltpu.VMEM((e, BF16_ROWS, d), jnp.bfloat16),
                        pltpu.SemaphoreType.DMA((2, e))])
    return pl.pallas_call(
        _dispatch_kernel,
        grid_spec=grid_spec,
        out_shape=jax.ShapeDtypeStruct((e, rows_padded, d), jnp.bfloat16),
        compiler_params=_cparams("arbitrary"),
        name="moe_dispatch",
    )(starts, slot, hn)


def _ffn_kernel(x_ref, wg_hbm, wu_hbm, wd_hbm, y_ref, wg_ref, wu_ref, wd_ref, sg_ref, su_ref, sd_ref,
                acc_ref, sem_ref, *, layer, steps):
    ee = pl.program_id(0)
    m = pl.program_id(1)
    n_chunks = wg_ref.shape[1]
    cur = ee % 2

    def chunk_copies(expert, c):
        cols = pl.ds(pl.multiple_of(c * FFN_CHUNK, FFN_CHUNK), FFN_CHUNK)
        return [pltpu.make_async_copy(wg_hbm.at[layer, expert, :, cols], sg_ref, sem_ref.at[0]),
                pltpu.make_async_copy(wu_hbm.at[layer, expert, :, cols], su_ref, sem_ref.at[1]),
                pltpu.make_async_copy(wd_hbm.at[layer, expert, cols, :], sd_ref, sem_ref.at[2])]

    def land(copies, half, c):
        for cp in copies:
            cp.wait()
        wg_ref[half, c] = sg_ref[...].astype(jnp.bfloat16)
        wu_ref[half, c] = su_ref[...].astype(jnp.bfloat16)
        wd_ref[half, c] = sd_ref[...].astype(jnp.bfloat16)

    @pl.when((ee == 0) & (m == 0))
    def _():
        def fetch(c, _):
            copies = chunk_copies(0, c)
            for cp in copies:
                cp.start()
            land(copies, 0, c)
            return 0
        lax.fori_loop(0, n_chunks, fetch, 0)

    per_step = -(-n_chunks // steps)
    first = m * per_step
    has_next = ee + 1 < pl.num_programs(0)
    prefetch = has_next & (first < n_chunks)

    @pl.when(prefetch)
    def _():
        for cp in chunk_copies(ee + 1, first):
            cp.start()

    x = x_ref[...]
    for c in range(n_chunks):
        g = jnp.dot(x, wg_ref[cur, c], preferred_element_type=jnp.float32)
        u = jnp.dot(x, wu_ref[cur, c], preferred_element_type=jnp.float32)
        h = (g * jax.nn.sigmoid(g) * u).astype(jnp.bfloat16)
        part = jnp.dot(h, wd_ref[cur, c], preferred_element_type=jnp.float32)
        if c == 0:
            acc_ref[...] = part
        else:
            acc_ref[...] += part
    y_ref[...] = acc_ref[...].astype(y_ref.dtype)

    @pl.when(prefetch)
    def _():
        land(chunk_copies(ee + 1, first), 1 - cur, first)

    for k in range(1, per_step):
        @pl.when(has_next & (first + k < n_chunks))
        def _():
            copies = chunk_copies(ee + 1, first + k)
            for cp in copies:
                cp.start()
            land(copies, 1 - cur, first + k)


def expert_ffn(xg, wg, wu, wd, layer, rows):
    e, _, d = xg.shape
    f = wg.shape[3]
    n_chunks = f // FFN_CHUNK
    return pl.pallas_call(
        functools.partial(_ffn_kernel, layer=layer, steps=rows // FFN_ROWS),
        grid=(e, rows // FFN_ROWS),
        in_specs=[pl.BlockSpec((None, FFN_ROWS, d), lambda ee, m: (ee, m, 0)),
                  pl.BlockSpec(memory_space=pl.ANY),
                  pl.BlockSpec(memory_space=pl.ANY),
                  pl.BlockSpec(memory_space=pl.ANY)],
        out_specs=pl.BlockSpec((None, FFN_ROWS, d), lambda ee, m: (ee, m, 0)),
        out_shape=jax.ShapeDtypeStruct((e, rows, d), jnp.bfloat16),
        scratch_shapes=[pltpu.VMEM((2, n_chunks, d, FFN_CHUNK), jnp.bfloat16),
                        pltpu.VMEM((2, n_chunks, d, FFN_CHUNK), jnp.bfloat16),
                        pltpu.VMEM((2, n_chunks, FFN_CHUNK, d), jnp.bfloat16),
                        pltpu.VMEM((d, FFN_CHUNK), jnp.float32),
                        pltpu.VMEM((d, FFN_CHUNK), jnp.float32),
                        pltpu.VMEM((FFN_CHUNK, d), jnp.float32),
                        pltpu.VMEM((FFN_ROWS, d), jnp.float32),
                        pltpu.SemaphoreType.DMA((3,))],
        compiler_params=_cparams("arbitrary", "arbitrary"),
        name="expert_ffn",
    )(xg, wg, wu, wd)


def _combine_kernel(starts_ref, slot_ref, gate_ref, x_ref, gain_ref, ys_ref, o_ref, buf_ref, sem_ref,
                    *, rows, final):
    i = pl.program_id(0)
    nt = pl.num_programs(0)
    tt, e = slot_ref.shape
    w = MOE_WIN
    d = x_ref.shape[1]
    b = i % 2

    def aligned(s):
        return (s // BF16_ROWS) * BF16_ROWS

    def window_start(ti, ee, p):
        return jnp.minimum(aligned(starts_ref[ti * e + ee]) + p * w, rows - w)

    def copies(ti, bb, p):
        return [pltpu.make_async_copy(
            ys_ref.at[ee, pl.ds(pl.multiple_of(window_start(ti, ee, p), BF16_ROWS), w), :],
            buf_ref.at[bb, pl.ds(ee * w, w), :],
            sem_ref.at[bb, ee]) for ee in range(e)]

    @pl.when(i == 0)
    def _():
        for c in copies(i, b, 0):
            c.start()

    @pl.when(i + 1 < nt)
    def _():
        for c in copies(i + 1, 1 - b, 0):
            c.start()

    n_pass = jnp.int32(1)
    for ee in range(e):
        n_pass = jnp.maximum(
            n_pass, (starts_ref[(i + 1) * e + ee] - aligned(starts_ref[i * e + ee]) + (w - 1)) // w)

    slot = slot_ref[...]
    gate = gate_ref[...]
    g_hi = gate.astype(jnp.bfloat16)
    g_lo = (gate - g_hi.astype(jnp.float32)).astype(jnp.bfloat16)
    expert = lax.broadcasted_iota(jnp.int32, (1, e), 1)
    spread = jnp.where(lax.broadcasted_iota(jnp.int32, (e, e * w), 1) // w
                       == lax.broadcasted_iota(jnp.int32, (e, e * w), 0), 1.0, 0.0).astype(jnp.bfloat16)
    g_hi_cols = jnp.dot(g_hi, spread, preferred_element_type=jnp.float32)
    g_lo_cols = jnp.dot(g_lo, spread, preferred_element_type=jnp.float32)
    col_in_window = (lax.broadcasted_iota(jnp.int32, (tt, e * w), 1) % w).astype(jnp.float32)

    def one_pass(p, acc):
        @pl.when(p > 0)
        def _():
            for c in copies(i, b, p):
                c.start()

        for c in copies(i, b, p):
            c.wait()

        lo = jnp.zeros((1, e), jnp.int32)
        ws = jnp.zeros((1, e), jnp.int32)
        for ee in range(e):
            lo = jnp.where(expert == ee, aligned(starts_ref[i * e + ee]) + p * w, lo)
            ws = jnp.where(expert == ee, window_start(i, ee, p), ws)
        ok = (slot >= lo) & (slot < lo + w)
        rel = jnp.where(ok, slot - ws, -1).astype(jnp.float32).astype(jnp.bfloat16)
        rel_cols = jnp.dot(rel, spread, preferred_element_type=jnp.float32)
        hit = rel_cols == col_in_window
        q_hi = jnp.where(hit, g_hi_cols, 0.0).astype(jnp.bfloat16)
        q_lo = jnp.where(hit, g_lo_cols, 0.0).astype(jnp.bfloat16)
        ys = buf_ref[b]
        return (acc + jnp.dot(q_hi, ys, preferred_element_type=jnp.float32)
                + jnp.dot(q_lo, ys, preferred_element_type=jnp.float32))

    moe = lax.fori_loop(0, n_pass, one_pass, jnp.zeros((tt, d), jnp.float32))
    x = x_ref[...] + moe
    if final:
        ms = jnp.mean(x * x, axis=-1, keepdims=True)
        x = x * lax.rsqrt(ms + RMS_EPS) * gain_ref[...]
    o_ref[...] = x


def combine(starts, slot_t, gate_t, x, gain, ys, final):
    t, e = slot_t.shape
    d = x.shape[1]
    rows = ys.shape[1]
    grid_spec = pltpu.PrefetchScalarGridSpec(
        num_scalar_prefetch=1,
        grid=(t // MOE_TILE,),
        in_specs=[pl.BlockSpec((MOE_TILE, e), lambda i, s: (i, 0)),
                  pl.BlockSpec((MOE_TILE, e), lambda i, s: (i, 0)),
                  pl.BlockSpec((MOE_TILE, d), lambda i, s: (i, 0)),
                  pl.BlockSpec((1, d), lambda i, s: (0, 0)),
                  pl.BlockSpec(memory_space=pl.ANY)],
        out_specs=pl.BlockSpec((MOE_TILE, d), lambda i, s: (i, 0)),
        scratch_shapes=[pltpu.VMEM((2, e * MOE_WIN, d), jnp.bfloat16),
                        pltpu.SemaphoreType.DMA((2, e))])
    return pl.pallas_call(
        functools.partial(_combine_kernel, rows=rows, final=final),
        grid_spec=grid_spec,
        out_shape=jax.ShapeDtypeStruct((t, d), jnp.float32),
        compiler_params=_cparams("arbitrary"),
        name="moe_combine",
    )(starts, slot_t, gate_t, x, gain.reshape(1, d), ys)


def moe_layer(x1, hn, aff, groups, wg, wu, wd, layer, gain, final):
    e, t = aff.shape
    slots, rowstarts = [], []
    base = 0
    for (t0, tg) in groups:
        cap = EC_CAPACITY_FACTOR * tg // e
        aff3 = aff[:, t0:t0 + tg].reshape(e, tg // LANES, LANES)
        s, r = select_tokens(aff3, cap, base)
        slots.append(s.reshape(e, tg))
        rowstarts.append(r[:, ::MOE_TILE // LANES, 0])
        base += cap
    rows = base
    slot = jnp.concatenate(slots, axis=1)
    starts = jnp.concatenate(rowstarts + [jnp.full((e, 1), rows, jnp.int32)], axis=1)
    starts = starts.T.reshape(-1)
    xg = dispatch(starts, slot, hn, rows + MOE_MAX_PASSES * MOE_WIN)
    ys = expert_ffn(xg, wg, wu, wd, layer, rows)
    return combine(starts, slot.T, aff.T, x1, gain, ys, final)


RET_CHUNK = 256


def _retention_kernel(reset_ref, q_ref, k_ref, v_ref, intra_ref, qdec_ref, kdec_ref, cdec_ref,
                      *rest, reverse):
    if reverse:
        yf_ref, g_ref, o_ref, state_ref = rest
    else:
        o_ref, state_ref = rest
    i = pl.program_id(0)

    @pl.when(reset_ref[i] == 1)
    def _():
        state_ref[...] = jnp.zeros_like(state_ref)

    dk, dv = RET_QK_DIM, RET_V_DIM
    for h in range(RET_HEADS):
        q = q_ref[:, h * dk:(h + 1) * dk]
        k = k_ref[:, h * dk:(h + 1) * dk]
        v = v_ref[:, h * dv:(h + 1) * dv]
        state = state_ref[h]
        s = lax.dot_general(q, k, (((1,), (1,)), ((), ())), preferred_element_type=jnp.float32)
        inner = (s * intra_ref[h]).astype(jnp.bfloat16)
        qd = (q.astype(jnp.float32) * qdec_ref[h]).astype(jnp.bfloat16)
        y = (jnp.dot(inner, v, preferred_element_type=jnp.float32)
             + jnp.dot(qd, state.astype(jnp.bfloat16), preferred_element_type=jnp.float32))
        kd = (k.astype(jnp.float32) * kdec_ref[h]).astype(jnp.bfloat16)
        state_ref[h] = state * cdec_ref[h] + lax.dot_general(
            kd, v, (((0,), (0,)), ((), ())), preferred_element_type=jnp.float32)
        if reverse:
            y = y + yf_ref[:, h * dv:(h + 1) * dv].astype(jnp.float32)
            y = y * lax.rsqrt(jnp.mean(y * y, axis=-1, keepdims=True) + RMS_EPS)
            g = g_ref[:, h * dv:(h + 1) * dv].astype(jnp.float32)
            y = g * jax.nn.sigmoid(g) * y
        o_ref[:, h * dv:(h + 1) * dv] = y.astype(o_ref.dtype)


def _decay_tables(log_gamma, reverse):
    c = RET_CHUNK
    lg = log_gamma.astype(jnp.float32)[:, None, None]
    pos = jnp.arange(c, dtype=jnp.float32)
    rel = pos[:, None] - pos[None, :]
    scale = RET_QK_DIM ** -0.5
    if reverse:
        intra = jnp.where(rel < 0, jnp.exp(lg * jnp.maximum(-rel, 0.0)[None]), 0.0)
        qdec = jnp.exp(lg * (c - pos)[None, :, None])
        kdec = jnp.exp(lg * pos[None, :, None])
    else:
        intra = jnp.where(rel >= 0, jnp.exp(lg * jnp.maximum(rel, 0.0)[None]), 0.0)
        qdec = jnp.exp(lg * (pos + 1.0)[None, :, None])
        kdec = jnp.exp(lg * (c - 1.0 - pos)[None, :, None])
    return intra * scale, qdec, kdec * scale, jnp.exp(lg * c)


def retention(proj, seq_lens, decay, reverse, y_fwd=None):
    t = proj.shape[0]
    c = RET_CHUNK
    nc = t // c
    bounds = np.cumsum([0] + [s // c for s in seq_lens])
    reset = np.zeros((nc,), np.int32)
    if reverse:
        reset[nc - bounds[1:]] = 1
        chunk = lambda i, r: nc - 1 - i
    else:
        reset[bounds[:-1]] = 1
        chunk = lambda i, r: i
    intra, qdec, kdec, cdec = _decay_tables(-jnp.exp(decay.astype(jnp.float32)), reverse)
    d, vw, h = D_MODEL, RET_V_WIDTH, RET_HEADS
    const = lambda shape: pl.BlockSpec(shape, lambda i, r: (0,) * len(shape))
    in_specs = [pl.BlockSpec((c, d), lambda i, r: (chunk(i, r), 0)),
                pl.BlockSpec((c, d), lambda i, r: (chunk(i, r), 1)),
                pl.BlockSpec((c, vw), lambda i, r: (chunk(i, r), 1)),
                const((h, c, c)), const((h, c, 1)), const((h, c, 1)), const((h, 1, 1))]
    args = [jnp.asarray(reset), proj, proj, proj, intra, qdec, kdec, cdec]
    if reverse:
        in_specs += [pl.BlockSpec((c, vw), lambda i, r: (chunk(i, r), 0)),
                     pl.BlockSpec((c, vw), lambda i, r: (chunk(i, r), 2))]
        args += [y_fwd, proj]
    grid_spec = pltpu.PrefetchScalarGridSpec(
        num_scalar_prefetch=1, grid=(nc,), in_specs=in_specs,
        out_specs=pl.BlockSpec((c, vw), lambda i, r: (chunk(i, r), 0)),
        scratch_shapes=[pltpu.VMEM((h, RET_QK_DIM, RET_V_DIM), jnp.float32)])
    return pl.pallas_call(
        functools.partial(_retention_kernel, reverse=reverse),
        grid_spec=grid_spec,
        out_shape=jax.ShapeDtypeStruct((t, vw), jnp.bfloat16),
        compiler_params=_cparams("arbitrary"),
        name="retention_bwd" if reverse else "retention_fwd",
    )(*args)


ATT_BLOCK = 1024
ATT_HALF = 64
ATT_Q = 128
ATT_DEINT = 4
assert tuple(d for _, d in DIL_PATTERNS) == (1, ATT_DEINT, ATT_DEINT * ATT_DEINT)
assert all(w == 2 * ATT_HALF * d for w, d in DIL_PATTERNS)


def _attention_kernel(vprev_ref, vnext_ref, q_ref, kp_ref, kc_ref, kn_ref, vp_ref, vc_ref, vn_ref,
                      bias_ref, o_ref, q4_ref, k4_ref, v4_ref, kedge_ref, vedge_ref, edge_ref,
                      num1_ref, m1_ref, l1_ref, num4_ref, m4_ref, l4_ref, onat_ref,
                      q16_ref, k16_ref, v16_ref, num16_ref, m16_ref, l16_ref):
    i = pl.program_id(0)
    b = ATT_BLOCK
    g = ATT_DEINT
    bq = b // g
    h = ATT_HALF
    half_lane = ATT_HEAD_DIM
    no_prev = vprev_ref[i] == 0
    no_next = vnext_ref[i] == 0
    scale = ATT_HEAD_DIM ** -0.5 * math.log2(math.e)
    kv_blocks = ((kp_ref, vp_ref), (kc_ref, vc_ref), (kn_ref, vn_ref))

    for c in range(g):
        q4_ref[c] = q_ref[pl.ds(c, bq, stride=g), :]
        for blk, (k_blk, v_blk) in enumerate(kv_blocks):
            k4_ref[c, blk * bq:(blk + 1) * bq] = k_blk[pl.ds(c, bq, stride=g), :]
            v4_ref[c, blk * bq:(blk + 1) * bq] = v_blk[pl.ds(c, bq, stride=g), :]

    nk1 = ATT_Q + 2 * h
    for edge_buf, (prv, cur, nxt) in ((kedge_ref, (kp_ref, kc_ref, kn_ref)),
                                      (vedge_ref, (vp_ref, vc_ref, vn_ref))):
        edge_buf[0, 0:h] = prv[b - h:b]
        edge_buf[0, h:nk1] = cur[0:nk1 - h]
        edge_buf[1, 0:nk1 - h] = cur[b - (nk1 - h):b]
        edge_buf[1, nk1 - h:nk1] = nxt[0:h]

    for br, (_, d) in enumerate(DIL_PATTERNS):
        nq = min(ATT_Q, b // d)
        nk = nq + 2 * h
        col = lax.broadcasted_iota(jnp.int32, (1, nk), 1)
        before = jnp.where((col < h) & no_prev, NEG_INF, 0.0)
        after = jnp.where((col >= nk - h) & no_next, NEG_INF, 0.0)
        if b // d == nq:
            before = before + after
        for hd in range(2):
            edge_ref[br, 0, hd, 0:nq, 0:nk] = bias_ref[br, hd, 0:nq, 0:nk] + before
            edge_ref[br, 1, hd, 0:nq, 0:nk] = bias_ref[br, hd, 0:nq, 0:nk] + after

    def bias_of(br, j, nsub, nq, nk):
        if j == 0:
            return [edge_ref[br, 0, hd, 0:nq, 0:nk] for hd in range(2)]
        if j == nsub - 1:
            return [edge_ref[br, 1, hd, 0:nq, 0:nk] for hd in range(2)]
        return [bias_ref[br, hd, 0:nq, 0:nk] for hd in range(2)]

    def scores(load_q, load_k, bias):
        q = load_q()
        nq = q.shape[0]
        first = lax.broadcasted_iota(jnp.int32, (nq, LANES), 1) < half_lane
        q = q * scale
        q2 = jnp.concatenate([jnp.where(first, q, 0.0), jnp.where(first, 0.0, q)], axis=0)
        s = lax.dot_general(q2.astype(jnp.bfloat16), load_k().astype(jnp.bfloat16),
                            (((1,), (1,)), ((), ())), preferred_element_type=jnp.float32)
        return s + jnp.concatenate(bias(), axis=0)

    def softmax(s):
        m = jnp.max(s, axis=-1, keepdims=True)
        p = jnp.exp2(s - m)
        return p.astype(jnp.bfloat16), m, jnp.sum(p, axis=-1, keepdims=True)

    def values(p, m, l, load_v, store):
        nq = p.shape[0] // 2
        first = lax.broadcasted_iota(jnp.int32, (nq, LANES), 1) < half_lane
        pv = jnp.dot(p, load_v().astype(jnp.bfloat16), preferred_element_type=jnp.float32)
        store(jnp.where(first, pv[:nq], pv[nq:]), jnp.where(first, m[:nq], m[nq:]),
              jnp.where(first, l[:nq], l[nq:]))

    units = []

    def store_to(num_r, m_r, l_r, idx):
        def store(num, m, l):
            num_r[idx] = num
            m_r[idx] = m
            l_r[idx] = l
        return store

    nq, nk, nsub = ATT_Q, nk1, b // ATT_Q
    for j in range(nsub):
        rows = slice(j * nq, (j + 1) * nq)
        keys = slice(j * nq - h, j * nq - h + nk)
        if j == 0:
            load_k, load_v = (lambda: kedge_ref[0]), (lambda: vedge_ref[0])
        elif j == nsub - 1:
            load_k, load_v = (lambda: kedge_ref[1]), (lambda: vedge_ref[1])
        else:
            load_k, load_v = (lambda keys=keys: kc_ref[keys]), (lambda keys=keys: vc_ref[keys])
        units.append(((lambda rows=rows: q_ref[rows]), load_k, load_v,
                      (lambda j=j, a=(nsub, nq, nk): bias_of(0, j, *a)),
                      store_to(num1_ref, m1_ref, l1_ref, rows)))

    nq = min(ATT_Q, bq)
    nk, nsub = nq + 2 * h, bq // nq
    for c in range(g):
        for j in range(nsub):
            rows = slice(j * nq, (j + 1) * nq)
            keys = slice(bq + j * nq - h, bq + j * nq - h + nk)
            units.append(((lambda c=c, rows=rows: q4_ref[c, rows]),
                          (lambda c=c, keys=keys: k4_ref[c, keys]),
                          (lambda c=c, keys=keys: v4_ref[c, keys]),
                          (lambda j=j, a=(nsub, nq, nk): bias_of(1, j, *a)),
                          store_to(num4_ref, m4_ref, l4_ref, (0, c, rows))))

    nq = bq // g
    nk = nq + 2 * h
    for c in range(g):
        for a in range(g):
            q16_ref[c, a] = q4_ref[c, pl.ds(a, nq, stride=g)]
            k16_ref[c, a] = k4_ref[c, pl.ds(a, nk, stride=g)]
            v16_ref[c, a] = v4_ref[c, pl.ds(a, nk, stride=g)]
            units.append(((lambda c=c, a=a: q16_ref[c, a]), (lambda c=c, a=a: k16_ref[c, a]),
                          (lambda c=c, a=a: v16_ref[c, a]),
                          (lambda a_=(1, nq, nk): bias_of(2, 0, *a_)),
                          store_to(num16_ref, m16_ref, l16_ref, (c, a))))

    s_prev = None
    sm_prev = None
    for t in range(len(units) + 2):
        s_new = scores(units[t][0], units[t][1], units[t][3]) if t < len(units) else None
        sm_new = softmax(s_prev) if s_prev is not None else None
        if sm_prev is not None:
            values(*sm_prev, units[t - 2][2], units[t - 2][4])
        s_prev, sm_prev = s_new, sm_new

    for c in range(g):
        for a in range(g):
            rows = pl.ds(a, nq, stride=g)
            num4_ref[1, c, rows] = num16_ref[c, a]
            m4_ref[1, c, rows] = m16_ref[c, a]
            l4_ref[1, c, rows] = l16_ref[c, a]

    for c in range(g):
        rows = pl.ds(c, bq, stride=g)
        ms = [m1_ref[rows], m4_ref[0, c], m4_ref[1, c]]
        ls = [l1_ref[rows], l4_ref[0, c], l4_ref[1, c]]
        nums = [num1_ref[rows], num4_ref[0, c], num4_ref[1, c]]
        m_max = jnp.maximum(jnp.maximum(ms[0], ms[1]), ms[2])
        wts = [jnp.exp2(mm - m_max) for mm in ms]
        den = wts[0] * ls[0] + wts[1] * ls[1] + wts[2] * ls[2]
        num = wts[0] * nums[0] + wts[1] * nums[1] + wts[2] * nums[2]
        onat_ref[rows] = num / den
    o_ref[...] = onat_ref[...].astype(o_ref.dtype)


def _alibi_bias():
    slopes = jnp.exp2(-8.0 * jnp.arange(1, ATT_HEADS + 1, dtype=jnp.float32) / ATT_HEADS)
    qi = jnp.arange(ATT_Q)
    ki = jnp.arange(ATT_Q + 2 * ATT_HALF) - ATT_HALF
    rel = jnp.abs(ki[None, :] - qi[:, None])
    dil = jnp.asarray([d for _, d in DIL_PATTERNS], jnp.float32)
    bias = -slopes[:, None, None, None] * (dil[None, :, None, None] * rel.astype(jnp.float32)[None, None])
    bias = jnp.where((rel <= ATT_HALF)[None, None], bias * math.log2(math.e), NEG_INF)
    return bias.reshape(ATT_HEADS // 2, 2, len(DIL_PATTERNS), *rel.shape).transpose(0, 2, 1, 3, 4)


def dilated_attention(qkv, seq_lens):
    t = qkv.shape[0]
    b = ATT_BLOCK
    g = ATT_DEINT
    nb = t // b
    bounds = np.cumsum([0] + [s // b for s in seq_lens])
    vprev = np.ones((nb,), np.int32)
    vnext = np.ones((nb,), np.int32)
    vprev[bounds[:-1]] = 0
    vnext[bounds[1:] - 1] = 0
    pairs = ATT_HEADS // 2
    blk = lambda which, off: pl.BlockSpec(
        (b, LANES), lambda i, hp, vp, vn: (jnp.clip(i + which, 0, nb - 1), off + hp))
    bias = _alibi_bias()
    grid_spec = pltpu.PrefetchScalarGridSpec(
        num_scalar_prefetch=2, grid=(nb, pairs),
        in_specs=[blk(0, 0),
                  blk(-1, pairs), blk(0, pairs), blk(1, pairs),
                  blk(-1, 2 * pairs), blk(0, 2 * pairs), blk(1, 2 * pairs),
                  pl.BlockSpec((None,) + bias.shape[1:], lambda i, hp, vp, vn: (hp, 0, 0, 0, 0))],
        out_specs=pl.BlockSpec((b, LANES), lambda i, hp, vp, vn: (i, hp)),
        scratch_shapes=[pltpu.VMEM((g, b // g, LANES), jnp.float32),
                        pltpu.VMEM((g, 3 * b // g, LANES), jnp.float32),
                        pltpu.VMEM((g, 3 * b // g, LANES), jnp.float32),
                        pltpu.VMEM((2, ATT_Q + 2 * ATT_HALF, LANES), jnp.float32),
                        pltpu.VMEM((2, ATT_Q + 2 * ATT_HALF, LANES), jnp.float32),
                        pltpu.VMEM((len(DIL_PATTERNS), 2) + bias.shape[2:], jnp.float32)]
        + [pltpu.VMEM((b, LANES), jnp.float32)] * 3
        + [pltpu.VMEM((2, g, b // g, LANES), jnp.float32)] * 3
        + [pltpu.VMEM((b, LANES), jnp.float32)]
        + [pltpu.VMEM((g, g, b // (g * g), LANES), jnp.float32),
           pltpu.VMEM((g, g, b // (g * g) + 2 * ATT_HALF, LANES), jnp.float32),
           pltpu.VMEM((g, g, b // (g * g) + 2 * ATT_HALF, LANES), jnp.float32)]
        + [pltpu.VMEM((g, g, b // (g * g), LANES), jnp.float32)] * 3)
    return pl.pallas_call(
        _attention_kernel,
        grid_spec=grid_spec,
        out_shape=jax.ShapeDtypeStruct((t, D_MODEL), jnp.bfloat16),
        compiler_params=_cparams("parallel", "arbitrary"),
        name="dilated_attention",
    )(jnp.asarray(vprev), jnp.asarray(vnext), qkv, qkv, qkv, qkv, qkv, qkv, qkv, bias)


PROJ_COLS = 1024


def kernel(x_prompt, x_sample, norm_mix, norm_ffn, norm_final, ret_w_in, ret_w_out, ret_decay_fwd,
           ret_decay_bwd, att_w_qkv, att_w_out, moe_router, moe_w_gate, moe_w_up, moe_w_down):
    d = x_prompt.shape[-1]
    depth = norm_mix.shape[0]
    tp = x_prompt.shape[0] * x_prompt.shape[1]
    ts = x_sample.shape[0] * x_sample.shape[1]
    seq_lens = [x_prompt.shape[1]] * x_prompt.shape[0] + [x_sample.shape[1]] * x_sample.shape[0]
    groups = ((0, tp), (tp, ts))
    bf16 = lambda w: w.astype(jnp.bfloat16)

    x = jnp.concatenate([x_prompt.reshape(tp, d), x_sample.reshape(ts, d)], axis=0)
    for i in range(depth):
        j = i // 2
        if i % 2 == 0:
            proj = norm_matmul(x, norm_mix[i], bf16(ret_w_in[j]), jnp.bfloat16, PROJ_COLS)
            y_fwd = retention(proj, seq_lens, ret_decay_fwd[j], False)
            z = retention(proj, seq_lens, ret_decay_bwd[j], True, y_fwd)
            w_out = ret_w_out[j]
        else:
            qkv = norm_matmul(x, norm_mix[i], bf16(att_w_qkv[j]), jnp.float32, PROJ_COLS)
            z = dilated_attention(qkv, seq_lens)
            w_out = att_w_out[j]
        x1, hn, aff = outproj_router(z, bf16(w_out), x, norm_ffn[i], moe_router[i])
        x = moe_layer(x1, hn, aff, groups, moe_w_gate, moe_w_up, moe_w_down, i,
                      norm_final, final=(i == depth - 1))
    return x[:tp].reshape(x_prompt.shape), x[tp:].reshape(x_sample.shape)
```

```python
import functools
import math

import jax
import jax.numpy as jnp
import numpy as np
from jax import lax
from jax.experimental import pallas as pl
from jax.experimental.pallas import tpu as pltpu

D_MODEL = 1024
RET_HEADS = 4
RET_QK_DIM = D_MODEL // RET_HEADS
RET_V_WIDTH = 2 * D_MODEL
RET_V_DIM = RET_V_WIDTH // RET_HEADS
ATT_HEADS = 16
ATT_HEAD_DIM = D_MODEL // ATT_HEADS
DIL_PATTERNS = ((128, 1), (512, 4), (2048, 16))
N_EXPERTS = 16
EC_CAPACITY_FACTOR = 2
RMS_EPS = 1e-6
NEG_INF = -1e30

LANES = 128
BF16_ROWS = 16
VMEM_LIMIT = 56 * 1024 * 1024

ROW_TILE = 512
MOE_TILE = 256
MOE_WIN = 64
MOE_MAX_PASSES = MOE_TILE // MOE_WIN + 1
FFN_ROWS = 512
FFN_CHUNK = 256


def _cparams(*sem):
    return pltpu.CompilerParams(dimension_semantics=sem, vmem_limit_bytes=VMEM_LIMIT)


def _part_layout(parts, tile):
    specs, firsts, first = [], [], 0
    for p in parts:
        n = p.shape[0] // tile
        specs.append(pl.BlockSpec((tile, p.shape[1]),
                                  lambda i, *_, first=first, n=n: (jnp.clip(i - first, 0, n - 1), 0)))
        firsts.append(first)
        first += n
    return specs, tuple(firsts), first


def _read_parts(i, refs, firsts):
    x = refs[0][...]
    for ref, first in zip(refs[1:], firsts[1:]):
        x = jnp.where(i >= first, ref[...], x)
    return x


def _norm_matmul_kernel(*refs, tn, firsts):
    x_refs, (g_ref, w_ref, o_ref) = refs[:len(firsts)], refs[len(firsts):]
    x = _read_parts(pl.program_id(0), x_refs, firsts)
    ms = jnp.mean(x * x, axis=-1, keepdims=True)
    hn = (x * lax.rsqrt(ms + RMS_EPS) * g_ref[...]).astype(jnp.bfloat16)
    for c in range(w_ref.shape[1] // tn):
        cols = slice(c * tn, (c + 1) * tn)
        o_ref[:, cols] = jnp.dot(hn, w_ref[:, cols],
                                 preferred_element_type=jnp.float32).astype(o_ref.dtype)


def norm_matmul(x_parts, g, w_bf16, out_dtype, tn):
    d, n = w_bf16.shape
    x_specs, firsts, tiles = _part_layout(x_parts, ROW_TILE)
    return pl.pallas_call(
        functools.partial(_norm_matmul_kernel, tn=tn, firsts=firsts),
        grid=(tiles,),
        in_specs=x_specs + [pl.BlockSpec((1, d), lambda i: (0, 0)),
                            pl.BlockSpec((d, n), lambda i: (0, 0), pipeline_mode=pl.Buffered(1))],
        out_specs=pl.BlockSpec((ROW_TILE, n), lambda i: (i, 0)),
        out_shape=jax.ShapeDtypeStruct((tiles * ROW_TILE, n), out_dtype),
        compiler_params=_cparams("parallel"),
        name="norm_matmul",
    )(*x_parts, g.reshape(1, d), w_bf16)


def _outproj_router_kernel(z_ref, w_ref, g_ref, wr_ref, *refs, firsts):
    x_refs, (x1_ref, hn_ref, aff_ref) = refs[:len(firsts)], refs[len(firsts):]
    x = _read_parts(pl.program_id(0), x_refs, firsts)
    x1 = x + jnp.dot(z_ref[...], w_ref[...], preferred_element_type=jnp.float32)
    x1_ref[...] = x1
    ms = jnp.mean(x1 * x1, axis=-1, keepdims=True)
    hn = x1 * lax.rsqrt(ms + RMS_EPS) * g_ref[...]
    hn_ref[...] = hn.astype(jnp.bfloat16)
    logits = lax.dot_general(wr_ref[...], hn, (((1,), (1,)), ((), ())),
                             precision=lax.Precision.HIGHEST,
                             preferred_element_type=jnp.float32)
    m = jnp.max(logits, axis=0, keepdims=True)
    p = jnp.exp(logits - m)
    aff_ref[...] = p / jnp.sum(p, axis=0, keepdims=True)


def outproj_router(z, w_bf16, x_parts, g, w_router):
    t, k = z.shape
    d, e = w_router.shape
    x_specs, firsts, tiles = _part_layout(x_parts, ROW_TILE)
    assert tiles * ROW_TILE == t
    return pl.pallas_call(
        functools.partial(_outproj_router_kernel, firsts=firsts),
        grid=(tiles,),
        in_specs=[pl.BlockSpec((ROW_TILE, k), lambda i: (i, 0)),
                  pl.BlockSpec((k, d), lambda i: (0, 0)),
                  pl.BlockSpec((1, d), lambda i: (0, 0)),
                  pl.BlockSpec((e, d), lambda i: (0, 0))] + x_specs,
        out_specs=[pl.BlockSpec((ROW_TILE, d), lambda i: (i, 0)),
                   pl.BlockSpec((ROW_TILE, d), lambda i: (i, 0)),
                   pl.BlockSpec((e, ROW_TILE), lambda i: (0, i))],
        out_shape=[jax.ShapeDtypeStruct((t, d), jnp.float32),
                   jax.ShapeDtypeStruct((t, d), jnp.bfloat16),
                   jax.ShapeDtypeStruct((e, t), jnp.float32)],
        compiler_params=_cparams("parallel"),
        name="outproj_router",
    )(z, w_bf16, g.reshape(1, d), w_router.T, *x_parts)


def _select_kernel(aff_ref, slot_ref, rowstart_ref, *, cap, base):
    e, r, _ = aff_ref.shape
    bits = pltpu.bitcast(aff_ref[...], jnp.int32)

    def count(mask):
        c = jnp.sum(jnp.where(mask, 1.0, 0.0), axis=2, keepdims=True)
        return jnp.sum(c, axis=1, keepdims=True)

    def search(i, thr):
        cand = thr | jnp.left_shift(jnp.int32(1), 30 - i)
        return jnp.where(count(bits >= cand) >= cap, cand, thr)

    thr = lax.fori_loop(0, 31, search, jnp.zeros((e, 1, 1), jnp.int32))
    gt = bits > thr
    eq = bits == thr
    need = cap - count(gt)

    row_i = lax.broadcasted_iota(jnp.int32, (LANES, LANES), 0)
    col_i = lax.broadcasted_iota(jnp.int32, (LANES, LANES), 1)
    upper = jnp.where(row_i <= col_i, 1.0, 0.0).astype(jnp.bfloat16)
    ones = jnp.ones((LANES, LANES), jnp.bfloat16)
    rr = lax.broadcasted_iota(jnp.int32, (r, r), 0)
    rc = lax.broadcasted_iota(jnp.int32, (r, r), 1)
    lower = jnp.where(rc < rr, 1.0, 0.0).astype(jnp.bfloat16)

    def excl_cumsum(mask):
        m = jnp.where(mask, 1.0, 0.0).astype(jnp.bfloat16).reshape(e * r, LANES)
        incl = jnp.dot(m, upper, preferred_element_type=jnp.float32)
        tot = jnp.dot(m, ones, preferred_element_type=jnp.float32)
        offs = []
        for ee in range(e):
            t_e = tot[ee * r:(ee + 1) * r].astype(jnp.bfloat16)
            offs.append(jnp.dot(lower, t_e, preferred_element_type=jnp.float32))
        off = jnp.concatenate(offs, axis=0)
        excl = incl - m.astype(jnp.float32) + off
        return excl.reshape(e, r, LANES), off.reshape(e, r, LANES)

    eq_rank, _ = excl_cumsum(eq)
    sel = gt | (eq & (eq_rank < need))
    pos, off = excl_cumsum(sel)
    slot_ref[...] = jnp.where(sel, pos.astype(jnp.int32) + base, -1)
    rowstart_ref[...] = off.astype(jnp.int32) + base


def select_tokens(aff3, cap, base):
    e, r, _ = aff3.shape
    return pl.pallas_call(
        functools.partial(_select_kernel, cap=cap, base=base),
        out_shape=[jax.ShapeDtypeStruct((e, r, LANES), jnp.int32),
                   jax.ShapeDtypeStruct((e, r, LANES), jnp.int32)],
        compiler_params=pltpu.CompilerParams(vmem_limit_bytes=VMEM_LIMIT),
        name="select_tokens",
    )(aff3)


def _dispatch_kernel(starts_ref, slot_ref, hn_ref, xg_ref, win_ref, carry_ref, sem_ref):
    i = pl.program_id(0)
    nt = pl.num_programs(0)
    e = slot_ref.shape[0]
    tt = slot_ref.shape[1]
    w = MOE_WIN
    buf = i % 2

    def aligned(s):
        return (s // BF16_ROWS) * BF16_ROWS

    a = [aligned(starts_ref[i * e + ee]) for ee in range(e)]
    end = [starts_ref[(i + 1) * e + ee] for ee in range(e)]
    n_pass = jnp.int32(1)
    for ee in range(e):
        n_pass = jnp.maximum(n_pass, (end[ee] - a[ee] + (w - 1)) // w)

    @pl.when(i == 0)
    def _():
        carry_ref[...] = jnp.zeros_like(carry_ref)
        win_ref[1, 0] = jnp.zeros(win_ref.shape[2:], win_ref.dtype)
        tail = [pltpu.make_async_copy(win_ref.at[1, 0], xg_ref.at[ee, pl.ds(r0, w), :], sem_ref.at[1, ee])
                for ee in range(e) for r0 in range(xg_ref.shape[1] - MOE_MAX_PASSES * w, xg_ref.shape[1], w)]
        for c in tail:
            c.start()
        for c in tail:
            c.wait()

    def copies(b, p):
        return [pltpu.make_async_copy(
            win_ref.at[b, ee],
            xg_ref.at[ee, pl.ds(pl.multiple_of(a[ee] + p * w, BF16_ROWS), w), :],
            sem_ref.at[b, ee]) for ee in range(e)]

    def wait_tile(b, src_i):
        for ee in range(e):
            pltpu.make_async_copy(win_ref.at[b, ee], xg_ref.at[ee, pl.ds(0, w), :],
                                  sem_ref.at[b, ee]).wait()

    hn = hn_ref[...]
    row = lax.broadcasted_iota(jnp.int32, (w, tt), 0)

    def one_pass(p, _):
        onehot = []
        for ee in range(e):
            rel = slot_ref[pl.ds(ee, 1), :] - (a[ee] + p * w)
            onehot.append(jnp.where(row == rel, 1.0, 0.0).astype(jnp.bfloat16))
        onehot = jnp.concatenate(onehot, axis=0)
        rows = jnp.dot(onehot, hn, preferred_element_type=jnp.float32)

        @pl.when(p > 0)
        def _():
            for c in copies(buf, p - 1):
                c.wait()

        for ee in range(e):
            r_e = rows[ee * w:(ee + 1) * w]
            head = r_e[:BF16_ROWS] + jnp.where(p == 0, carry_ref[ee].astype(jnp.float32), 0.0)
            win_ref[buf, ee, pl.ds(0, BF16_ROWS), :] = head.astype(jnp.bfloat16)
            win_ref[buf, ee, pl.ds(BF16_ROWS, w - BF16_ROWS), :] = r_e[BF16_ROWS:].astype(jnp.bfloat16)
            nxt = aligned(end[ee]) - (a[ee] + p * w)

            @pl.when((nxt >= 0) & (nxt < w))
            def _():
                carry_ref[ee] = win_ref[buf, ee, pl.ds(pl.multiple_of(nxt, BF16_ROWS), BF16_ROWS), :]

            @pl.when((p == n_pass - 1) & (nxt >= w))
            def _():
                carry_ref[ee] = jnp.zeros((BF16_ROWS, carry_ref.shape[2]), carry_ref.dtype)

        @pl.when((p == 0) & (i > 0))
        def _():
            wait_tile(1 - buf, i - 1)

        for c in copies(buf, p):
            c.start()
        return 0

    lax.fori_loop(0, n_pass, one_pass, 0)

    @pl.when(i == nt - 1)
    def _():
        wait_tile(buf, i)


def dispatch(starts, slot, hn, rows_padded):
    e, t = slot.shape
    d = hn.shape[1]
    grid_spec = pltpu.PrefetchScalarGridSpec(
        num_scalar_prefetch=1,
        grid=(t // MOE_TILE,),
        in_specs=[pl.BlockSpec((e, MOE_TILE), lambda i, s: (0, i)),
                  pl.BlockSpec((MOE_TILE, d), lambda i, s: (i, 0))],
        out_specs=pl.BlockSpec(memory_space=pl.ANY),
        scratch_shapes=[pltpu.VMEM((2, e, MOE_WIN, d), jnp.bfloat16),
                        pltpu.VMEM((e, BF16_ROWS, d), jnp.bfloat16),
                        pltpu.SemaphoreType.DMA((2, e))])
    return pl.pallas_call(
        _dispatch_kernel,
        grid_spec=grid_spec,
        out_shape=jax.ShapeDtypeStruct((e, rows_padded, d), jnp.bfloat16),
        compiler_params=_cparams("arbitrary"),
        name="moe_dispatch",
    )(starts, slot, hn)


FFN_SLABS = 16


def _ffn_kernel(x_ref, wg_hbm, wu_hbm, wd_hbm, y_ref, wg_ref, wu_ref, wd_ref, sg_ref, su_ref, sd_ref,
                acc_ref, sem_ref, *, layer, steps):
    ee = pl.program_id(0)
    m = pl.program_id(1)
    d, f = x_ref.shape[1], wg_ref.shape[2]
    in_rows, hid_rows = sg_ref.shape[0], sd_ref.shape[0]
    cur = ee % 2

    def slab_copies(expert, c):
        r_in = pl.ds(pl.multiple_of(c * in_rows, BF16_ROWS), in_rows)
        r_hid = pl.ds(pl.multiple_of(c * hid_rows, BF16_ROWS), hid_rows)
        return [pltpu.make_async_copy(wg_hbm.at[layer, expert, r_in, :], sg_ref, sem_ref.at[0]),
                pltpu.make_async_copy(wu_hbm.at[layer, expert, r_in, :], su_ref, sem_ref.at[1]),
                pltpu.make_async_copy(wd_hbm.at[layer, expert, r_hid, :], sd_ref, sem_ref.at[2])]

    def land(copies, half, c):
        for cp in copies:
            cp.wait()
        r_in = pl.ds(pl.multiple_of(c * in_rows, BF16_ROWS), in_rows)
        r_hid = pl.ds(pl.multiple_of(c * hid_rows, BF16_ROWS), hid_rows)
        wg_ref[half, r_in, :] = sg_ref[...].astype(jnp.bfloat16)
        wu_ref[half, r_in, :] = su_ref[...].astype(jnp.bfloat16)
        wd_ref[half, r_hid, :] = sd_ref[...].astype(jnp.bfloat16)

    @pl.when((ee == 0) & (m == 0))
    def _():
        def fetch(c, _):
            copies = slab_copies(0, c)
            for cp in copies:
                cp.start()
            land(copies, 0, c)
            return 0
        lax.fori_loop(0, FFN_SLABS, fetch, 0)

    per_step = -(-FFN_SLABS // steps)
    first = m * per_step
    has_next = ee + 1 < pl.num_programs(0)
    prefetch = has_next & (first < FFN_SLABS)
    src_expert = jnp.minimum(ee + 1, pl.num_programs(0) - 1)
    copies = slab_copies(src_expert, jnp.minimum(first, FFN_SLABS - 1))
    for cp in copies:
        cp.start()

    x = x_ref[...]
    n_chunks = f // FFN_CHUNK
    for c in range(n_chunks):
        if c == n_chunks // 2:
            land(copies, 1 - cur, jnp.where(prefetch, first, FFN_SLABS))
        cols = slice(c * FFN_CHUNK, (c + 1) * FFN_CHUNK)
        g = jnp.dot(x, wg_ref[cur, 0:d, cols], preferred_element_type=jnp.float32)
        u = jnp.dot(x, wu_ref[cur, 0:d, cols], preferred_element_type=jnp.float32)
        h = (g * jax.nn.sigmoid(g) * u).astype(jnp.bfloat16)
        part = jnp.dot(h, wd_ref[cur, cols, :], preferred_element_type=jnp.float32)
        if c == 0:
            acc_ref[...] = part
        else:
            acc_ref[...] += part
    y_ref[...] = acc_ref[...].astype(y_ref.dtype)

    for k in range(1, per_step):
        @pl.when(has_next & (first + k < FFN_SLABS))
        def _():
            more = slab_copies(ee + 1, first + k)
            for cp in more:
                cp.start()
            land(more, 1 - cur, first + k)


def expert_ffn(xg, wg, wu, wd, layer, rows):
    e, _, d = xg.shape
    f = wg.shape[3]
    in_rows, hid_rows = d // FFN_SLABS, f // FFN_SLABS
    assert in_rows % BF16_ROWS == 0 and hid_rows % BF16_ROWS == 0 and f % FFN_CHUNK == 0
    return pl.pallas_call(
        functools.partial(_ffn_kernel, layer=layer, steps=rows // FFN_ROWS),
        grid=(e, rows // FFN_ROWS),
        in_specs=[pl.BlockSpec((None, FFN_ROWS, d), lambda ee, m: (ee, m, 0)),
                  pl.BlockSpec(memory_space=pl.ANY),
                  pl.BlockSpec(memory_space=pl.ANY),
                  pl.BlockSpec(memory_space=pl.ANY)],
        out_specs=pl.BlockSpec((None, FFN_ROWS, d), lambda ee, m: (ee, m, 0)),
        out_shape=jax.ShapeDtypeStruct((e, rows, d), jnp.bfloat16),
        scratch_shapes=[pltpu.VMEM((2, d + in_rows, f), jnp.bfloat16),
                        pltpu.VMEM((2, d + in_rows, f), jnp.bfloat16),
                        pltpu.VMEM((2, f + hid_rows, d), jnp.bfloat16),
                        pltpu.VMEM((in_rows, f), jnp.float32),
                        pltpu.VMEM((in_rows, f), jnp.float32),
                        pltpu.VMEM((hid_rows, d), jnp.float32),
                        pltpu.VMEM((FFN_ROWS, d), jnp.float32),
                        pltpu.SemaphoreType.DMA((3,))],
        compiler_params=_cparams("arbitrary", "arbitrary"),
        name="expert_ffn",
    )(xg, wg, wu, wd)


def _combine_kernel(starts_ref, slot_ref, gate_ref, x_ref, gain_ref, ys_ref, *refs, rows, final, out_tiles):
    o_refs, (buf_ref, sem_ref) = refs[:len(out_tiles)], refs[len(out_tiles):]
    i = pl.program_id(0)
    nt = pl.num_programs(0)
    tt, e = slot_ref.shape
    w = MOE_WIN
    d = x_ref.shape[1]
    b = i % 2

    def aligned(s):
        return (s // BF16_ROWS) * BF16_ROWS

    def window_start(ti, ee, p):
        return jnp.minimum(aligned(starts_ref[ti * e + ee]) + p * w, rows - w)

    def copies(ti, bb, p):
        return [pltpu.make_async_copy(
            ys_ref.at[ee, pl.ds(pl.multiple_of(window_start(ti, ee, p), BF16_ROWS), w), :],
            buf_ref.at[bb, pl.ds(ee * w, w), :],
            sem_ref.at[bb, ee]) for ee in range(e)]

    @pl.when(i == 0)
    def _():
        for c in copies(i, b, 0):
            c.start()

    @pl.when(i + 1 < nt)
    def _():
        for c in copies(i + 1, 1 - b, 0):
            c.start()

    n_pass = jnp.int32(1)
    for ee in range(e):
        n_pass = jnp.maximum(
            n_pass, (starts_ref[(i + 1) * e + ee] - aligned(starts_ref[i * e + ee]) + (w - 1)) // w)

    slot = slot_ref[...]
    gate = gate_ref[...]
    g_hi = gate.astype(jnp.bfloat16)
    g_lo = (gate - g_hi.astype(jnp.float32)).astype(jnp.bfloat16)
    expert = lax.broadcasted_iota(jnp.int32, (1, e), 1)
    spread = jnp.where(lax.broadcasted_iota(jnp.int32, (e, e * w), 1) // w
                       == lax.broadcasted_iota(jnp.int32, (e, e * w), 0), 1.0, 0.0).astype(jnp.bfloat16)
    g_hi_cols = jnp.dot(g_hi, spread, preferred_element_type=jnp.float32)
    g_lo_cols = jnp.dot(g_lo, spread, preferred_element_type=jnp.float32)
    col_in_window = (lax.broadcasted_iota(jnp.int32, (tt, e * w), 1) % w).astype(jnp.float32)

    def one_pass(p, acc):
        @pl.when(p > 0)
        def _():
            for c in copies(i, b, p):
                c.start()

        for c in copies(i, b, p):
            c.wait()

        lo = jnp.zeros((1, e), jnp.int32)
        ws = jnp.zeros((1, e), jnp.int32)
        for ee in range(e):
            lo = jnp.where(expert == ee, aligned(starts_ref[i * e + ee]) + p * w, lo)
            ws = jnp.where(expert == ee, window_start(i, ee, p), ws)
        ok = (slot >= lo) & (slot < lo + w)
        rel = jnp.where(ok, slot - ws, -1).astype(jnp.float32).astype(jnp.bfloat16)
        rel_cols = jnp.dot(rel, spread, preferred_element_type=jnp.float32)
        hit = rel_cols == col_in_window
        q_hi = jnp.where(hit, g_hi_cols, 0.0).astype(jnp.bfloat16)
        q_lo = jnp.where(hit, g_lo_cols, 0.0).astype(jnp.bfloat16)
        ys = buf_ref[b]
        return (acc + jnp.dot(q_hi, ys, preferred_element_type=jnp.float32)
                + jnp.dot(q_lo, ys, preferred_element_type=jnp.float32))

    moe = lax.fori_loop(0, n_pass, one_pass, jnp.zeros((tt, d), jnp.float32))
    x = x_ref[...] + moe
    if final:
        ms = jnp.mean(x * x, axis=-1, keepdims=True)
        x = x * lax.rsqrt(ms + RMS_EPS) * gain_ref[...]
    first = 0
    for o_ref, n in zip(o_refs, out_tiles):
        @pl.when((i >= first) & (i < first + n))
        def _(o_ref=o_ref):
            o_ref[...] = x
        first += n


def combine(starts, slot_t, gate_t, x, gain, ys, final, out_rows):
    t, e = slot_t.shape
    d = x.shape[1]
    rows = ys.shape[1]
    out_tiles = tuple(r // MOE_TILE for r in out_rows)
    assert sum(out_rows) == t
    out_specs, first = [], 0
    for n in out_tiles:
        out_specs.append(pl.BlockSpec((MOE_TILE, d),
                                      lambda i, s, first=first, n=n: (jnp.clip(i - first, 0, n - 1), 0)))
        first += n
    grid_spec = pltpu.PrefetchScalarGridSpec(
        num_scalar_prefetch=1,
        grid=(t // MOE_TILE,),
        in_specs=[pl.BlockSpec((MOE_TILE, e), lambda i, s: (i, 0)),
                  pl.BlockSpec((MOE_TILE, e), lambda i, s: (i, 0)),
                  pl.BlockSpec((MOE_TILE, d), lambda i, s: (i, 0)),
                  pl.BlockSpec((1, d), lambda i, s: (0, 0)),
                  pl.BlockSpec(memory_space=pl.ANY)],
        out_specs=out_specs,
        scratch_shapes=[pltpu.VMEM((2, e * MOE_WIN, d), jnp.bfloat16),
                        pltpu.SemaphoreType.DMA((2, e))])
    return pl.pallas_call(
        functools.partial(_combine_kernel, rows=rows, final=final, out_tiles=out_tiles),
        grid_spec=grid_spec,
        out_shape=[jax.ShapeDtypeStruct((r, d), jnp.float32) for r in out_rows],
        compiler_params=_cparams("arbitrary"),
        name="moe_combine",
    )(starts, slot_t, gate_t, x, gain.reshape(1, d), ys)


def moe_layer(x1, hn, aff, groups, wg, wu, wd, layer, gain, final, out_rows):
    e, t = aff.shape
    slots, rowstarts = [], []
    base = 0
    for (t0, tg) in groups:
        cap = EC_CAPACITY_FACTOR * tg // e
        aff3 = aff[:, t0:t0 + tg].reshape(e, tg // LANES, LANES)
        s, r = select_tokens(aff3, cap, base)
        slots.append(s.reshape(e, tg))
        rowstarts.append(r[:, ::MOE_TILE // LANES, 0])
        base += cap
    rows = base
    slot = jnp.concatenate(slots, axis=1)
    starts = jnp.concatenate(rowstarts + [jnp.full((e, 1), rows, jnp.int32)], axis=1)
    starts = starts.T.reshape(-1)
    xg = dispatch(starts, slot, hn, rows + MOE_MAX_PASSES * MOE_WIN)
    ys = expert_ffn(xg, wg, wu, wd, layer, rows)
    return tuple(combine(starts, slot.T, aff.T, x1, gain, ys, final, out_rows))


RET_CHUNK = 256


def _retention_kernel(reset_ref, q_ref, k_ref, v_ref, intra_ref, qdec_ref, kdec_ref, cdec_ref,
                      *rest, reverse):
    if reverse:
        yf_ref, g_ref, o_ref, state_ref = rest
    else:
        o_ref, state_ref = rest
    i = pl.program_id(0)

    @pl.when(reset_ref[i] == 1)
    def _():
        state_ref[...] = jnp.zeros_like(state_ref)

    dk, dv = RET_QK_DIM, RET_V_DIM
    for h in range(RET_HEADS):
        q = q_ref[:, h * dk:(h + 1) * dk]
        k = k_ref[:, h * dk:(h + 1) * dk]
        v = v_ref[:, h * dv:(h + 1) * dv]
        state = state_ref[h]
        s = lax.dot_general(q, k, (((1,), (1,)), ((), ())), preferred_element_type=jnp.float32)
        inner = (s * intra_ref[h]).astype(jnp.bfloat16)
        qd = (q.astype(jnp.float32) * qdec_ref[h]).astype(jnp.bfloat16)
        y = (jnp.dot(inner, v, preferred_element_type=jnp.float32)
             + jnp.dot(qd, state.astype(jnp.bfloat16), preferred_element_type=jnp.float32))
        kd = (k.astype(jnp.float32) * kdec_ref[h]).astype(jnp.bfloat16)
        state_ref[h] = state * cdec_ref[h] + lax.dot_general(
            kd, v, (((0,), (0,)), ((), ())), preferred_element_type=jnp.float32)
        if reverse:
            y = y + yf_ref[:, h * dv:(h + 1) * dv].astype(jnp.float32)
            y = y * lax.rsqrt(jnp.mean(y * y, axis=-1, keepdims=True) + RMS_EPS)
            g = g_ref[:, h * dv:(h + 1) * dv].astype(jnp.float32)
            y = g * jax.nn.sigmoid(g) * y
        o_ref[:, h * dv:(h + 1) * dv] = y.astype(o_ref.dtype)


def _decay_tables(log_gamma, reverse):
    c = RET_CHUNK
    lg = log_gamma.astype(jnp.float32)[:, None, None]
    pos = jnp.arange(c, dtype=jnp.float32)
    rel = pos[:, None] - pos[None, :]
    scale = RET_QK_DIM ** -0.5
    if reverse:
        intra = jnp.where(rel < 0, jnp.exp(lg * jnp.maximum(-rel, 0.0)[None]), 0.0)
        qdec = jnp.exp(lg * (c - pos)[None, :, None])
        kdec = jnp.exp(lg * pos[None, :, None])
    else:
        intra = jnp.where(rel >= 0, jnp.exp(lg * jnp.maximum(rel, 0.0)[None]), 0.0)
        qdec = jnp.exp(lg * (pos + 1.0)[None, :, None])
        kdec = jnp.exp(lg * (c - 1.0 - pos)[None, :, None])
    return intra * scale, qdec, kdec * scale, jnp.exp(lg * c)


def retention(proj, seq_lens, decay, reverse, y_fwd=None):
    t = proj.shape[0]
    c = RET_CHUNK
    nc = t // c
    bounds = np.cumsum([0] + [s // c for s in seq_lens])
    reset = np.zeros((nc,), np.int32)
    if reverse:
        reset[nc - bounds[1:]] = 1
        chunk = lambda i, r: nc - 1 - i
    else:
        reset[bounds[:-1]] = 1
        chunk = lambda i, r: i
    intra, qdec, kdec, cdec = _decay_tables(-jnp.exp(decay.astype(jnp.float32)), reverse)
    d, vw, h = D_MODEL, RET_V_WIDTH, RET_HEADS
    const = lambda shape: pl.BlockSpec(shape, lambda i, r: (0,) * len(shape))
    in_specs = [pl.BlockSpec((c, d), lambda i, r: (chunk(i, r), 0)),
                pl.BlockSpec((c, d), lambda i, r: (chunk(i, r), 1)),
                pl.BlockSpec((c, vw), lambda i, r: (chunk(i, r), 1)),
                const((h, c, c)), const((h, c, 1)), const((h, c, 1)), const((h, 1, 1))]
    args = [jnp.asarray(reset), proj, proj, proj, intra, qdec, kdec, cdec]
    if reverse:
        in_specs += [pl.BlockSpec((c, vw), lambda i, r: (chunk(i, r), 0)),
                     pl.BlockSpec((c, vw), lambda i, r: (chunk(i, r), 2))]
        args += [y_fwd, proj]
    grid_spec = pltpu.PrefetchScalarGridSpec(
        num_scalar_prefetch=1, grid=(nc,), in_specs=in_specs,
        out_specs=pl.BlockSpec((c, vw), lambda i, r: (chunk(i, r), 0)),
        scratch_shapes=[pltpu.VMEM((h, RET_QK_DIM, RET_V_DIM), jnp.float32)])
    return pl.pallas_call(
        functools.partial(_retention_kernel, reverse=reverse),
        grid_spec=grid_spec,
        out_shape=jax.ShapeDtypeStruct((t, vw), jnp.bfloat16),
        compiler_params=_cparams("arbitrary"),
        name="retention_bwd" if reverse else "retention_fwd",
    )(*args)


ATT_BLOCK = 1024
ATT_HALF = 64
ATT_Q = 128
ATT_DEINT = 4
assert tuple(d for _, d in DIL_PATTERNS) == (1, ATT_DEINT, ATT_DEINT * ATT_DEINT)
assert all(w == 2 * ATT_HALF * d for w, d in DIL_PATTERNS)


def _attention_kernel(vprev_ref, vnext_ref, q_ref, kp_ref, kc_ref, kn_ref, vp_ref, vc_ref, vn_ref,
                      bias_ref, o_ref, q4_ref, k4_ref, v4_ref, kedge_ref, vedge_ref, edge_ref,
                      num1_ref, m1_ref, l1_ref, num4_ref, m4_ref, l4_ref, onat_ref,
                      q16_ref, k16_ref, v16_ref, num16_ref, m16_ref, l16_ref):
    i = pl.program_id(0)
    b = ATT_BLOCK
    g = ATT_DEINT
    bq = b // g
    h = ATT_HALF
    half_lane = ATT_HEAD_DIM
    no_prev = vprev_ref[i] == 0
    no_next = vnext_ref[i] == 0
    scale = ATT_HEAD_DIM ** -0.5 * math.log2(math.e)
    kv_blocks = ((kp_ref, vp_ref), (kc_ref, vc_ref), (kn_ref, vn_ref))

    for c in range(g):
        q4_ref[c] = q_ref[pl.ds(c, bq, stride=g), :]
        for blk, (k_blk, v_blk) in enumerate(kv_blocks):
            k4_ref[c, blk * bq:(blk + 1) * bq] = k_blk[pl.ds(c, bq, stride=g), :]
            v4_ref[c, blk * bq:(blk + 1) * bq] = v_blk[pl.ds(c, bq, stride=g), :]

    nk1 = ATT_Q + 2 * h
    for edge_buf, (prv, cur, nxt) in ((kedge_ref, (kp_ref, kc_ref, kn_ref)),
                                      (vedge_ref, (vp_ref, vc_ref, vn_ref))):
        edge_buf[0, 0:h] = prv[b - h:b]
        edge_buf[0, h:nk1] = cur[0:nk1 - h]
        edge_buf[1, 0:nk1 - h] = cur[b - (nk1 - h):b]
        edge_buf[1, nk1 - h:nk1] = nxt[0:h]

    for br, (_, d) in enumerate(DIL_PATTERNS):
        nq = min(ATT_Q, b // d)
        nk = nq + 2 * h
        col = lax.broadcasted_iota(jnp.int32, (1, nk), 1)
        before = jnp.where((col < h) & no_prev, NEG_INF, 0.0)
        after = jnp.where((col >= nk - h) & no_next, NEG_INF, 0.0)
        if b // d == nq:
            before = before + after
        for hd in range(2):
            edge_ref[br, 0, hd, 0:nq, 0:nk] = bias_ref[br, hd, 0:nq, 0:nk] + before
            edge_ref[br, 1, hd, 0:nq, 0:nk] = bias_ref[br, hd, 0:nq, 0:nk] + after

    def bias_of(br, j, nsub, nq, nk):
        if j == 0:
            return [edge_ref[br, 0, hd, 0:nq, 0:nk] for hd in range(2)]
        if j == nsub - 1:
            return [edge_ref[br, 1, hd, 0:nq, 0:nk] for hd in range(2)]
        return [bias_ref[br, hd, 0:nq, 0:nk] for hd in range(2)]

    def scores(load_q, load_k, bias):
        q = load_q()
        nq = q.shape[0]
        first = lax.broadcasted_iota(jnp.int32, (nq, LANES), 1) < half_lane
        q = q * scale
        q2 = jnp.concatenate([jnp.where(first, q, 0.0), jnp.where(first, 0.0, q)], axis=0)
        s = lax.dot_general(q2.astype(jnp.bfloat16), load_k().astype(jnp.bfloat16),
                            (((1,), (1,)), ((), ())), preferred_element_type=jnp.float32)
        return s + jnp.concatenate(bias(), axis=0)

    def softmax(s):
        m = jnp.max(s, axis=-1, keepdims=True)
        p = jnp.exp2(s - m)
        return p.astype(jnp.bfloat16), m, jnp.sum(p, axis=-1, keepdims=True)

    def values(p, m, l, load_v, store):
        nq = p.shape[0] // 2
        first = lax.broadcasted_iota(jnp.int32, (nq, LANES), 1) < half_lane
        pv = jnp.dot(p, load_v().astype(jnp.bfloat16), preferred_element_type=jnp.float32)
        store(jnp.where(first, pv[:nq], pv[nq:]), jnp.where(first, m[:nq], m[nq:]),
              jnp.where(first, l[:nq], l[nq:]))

    units = []

    def store_to(num_r, m_r, l_r, idx):
        def store(num, m, l):
            num_r[idx] = num
            m_r[idx] = m
            l_r[idx] = l
        return store

    nq, nk, nsub = ATT_Q, nk1, b // ATT_Q
    for j in range(nsub):
        rows = slice(j * nq, (j + 1) * nq)
        keys = slice(j * nq - h, j * nq - h + nk)
        if j == 0:
            load_k, load_v = (lambda: kedge_ref[0]), (lambda: vedge_ref[0])
        elif j == nsub - 1:
            load_k, load_v = (lambda: kedge_ref[1]), (lambda: vedge_ref[1])
        else:
            load_k, load_v = (lambda keys=keys: kc_ref[keys]), (lambda keys=keys: vc_ref[keys])
        units.append(((lambda rows=rows: q_ref[rows]), load_k, load_v,
                      (lambda j=j, a=(nsub, nq, nk): bias_of(0, j, *a)),
                      store_to(num1_ref, m1_ref, l1_ref, rows)))

    nq = min(ATT_Q, bq)
    nk, nsub = nq + 2 * h, bq // nq
    for c in range(g):
        for j in range(nsub):
            rows = slice(j * nq, (j + 1) * nq)
            keys = slice(bq + j * nq - h, bq + j * nq - h + nk)
            units.append(((lambda c=c, rows=rows: q4_ref[c, rows]),
                          (lambda c=c, keys=keys: k4_ref[c, keys]),
                          (lambda c=c, keys=keys: v4_ref[c, keys]),
                          (lambda j=j, a=(nsub, nq, nk): bias_of(1, j, *a)),
                          store_to(num4_ref, m4_ref, l4_ref, (0, c, rows))))

    nq = bq // g
    nk = nq + 2 * h
    for c in range(g):
        for a in range(g):
            q16_ref[c, a] = q4_ref[c, pl.ds(a, nq, stride=g)]
            k16_ref[c, a] = k4_ref[c, pl.ds(a, nk, stride=g)]
            v16_ref[c, a] = v4_ref[c, pl.ds(a, nk, stride=g)]
            units.append(((lambda c=c, a=a: q16_ref[c, a]), (lambda c=c, a=a: k16_ref[c, a]),
                          (lambda c=c, a=a: v16_ref[c, a]),
                          (lambda a_=(1, nq, nk): bias_of(2, 0, *a_)),
                          store_to(num16_ref, m16_ref, l16_ref, (c, a))))

    s_prev = None
    sm_prev = None
    for t in range(len(units) + 2):
        s_new = scores(units[t][0], units[t][1], units[t][3]) if t < len(units) else None
        sm_new = softmax(s_prev) if s_prev is not None else None
        if sm_prev is not None:
            values(*sm_prev, units[t - 2][2], units[t - 2][4])
        s_prev, sm_prev = s_new, sm_new

    for c in range(g):
        for a in range(g):
            rows = pl.ds(a, nq, stride=g)
            num4_ref[1, c, rows] = num16_ref[c, a]
            m4_ref[1, c, rows] = m16_ref[c, a]
            l4_ref[1, c, rows] = l16_ref[c, a]

    for c in range(g):
        rows = pl.ds(c, bq, stride=g)
        ms = [m1_ref[rows], m4_ref[0, c], m4_ref[1, c]]
        ls = [l1_ref[rows], l4_ref[0, c], l4_ref[1, c]]
        nums = [num1_ref[rows], num4_ref[0, c], num4_ref[1, c]]
        m_max = jnp.maximum(jnp.maximum(ms[0], ms[1]), ms[2])
        wts = [jnp.exp2(mm - m_max) for mm in ms]
        den = wts[0] * ls[0] + wts[1] * ls[1] + wts[2] * ls[2]
        num = wts[0] * nums[0] + wts[1] * nums[1] + wts[2] * nums[2]
        onat_ref[rows] = num / den
    o_ref[...] = onat_ref[...].astype(o_ref.dtype)


def _alibi_bias():
    slopes = jnp.exp2(-8.0 * jnp.arange(1, ATT_HEADS + 1, dtype=jnp.float32) / ATT_HEADS)
    qi = jnp.arange(ATT_Q)
    ki = jnp.arange(ATT_Q + 2 * ATT_HALF) - ATT_HALF
    rel = jnp.abs(ki[None, :] - qi[:, None])
    dil = jnp.asarray([d for _, d in DIL_PATTERNS], jnp.float32)
    bias = -slopes[:, None, None, None] * (dil[None, :, None, None] * rel.astype(jnp.float32)[None, None])
    bias = jnp.where((rel <= ATT_HALF)[None, None], bias * math.log2(math.e), NEG_INF)
    return bias.reshape(ATT_HEADS // 2, 2, len(DIL_PATTERNS), *rel.shape).transpose(0, 2, 1, 3, 4)


def dilated_attention(qkv, seq_lens):
    t = qkv.shape[0]
    b = ATT_BLOCK
    g = ATT_DEINT
    nb = t // b
    bounds = np.cumsum([0] + [s // b for s in seq_lens])
    vprev = np.ones((nb,), np.int32)
    vnext = np.ones((nb,), np.int32)
    vprev[bounds[:-1]] = 0
    vnext[bounds[1:] - 1] = 0
    pairs = ATT_HEADS // 2
    blk = lambda which, off: pl.BlockSpec(
        (b, LANES), lambda i, hp, vp, vn: (jnp.clip(i + which, 0, nb - 1), off + hp))
    bias = _alibi_bias()
    grid_spec = pltpu.PrefetchScalarGridSpec(
        num_scalar_prefetch=2, grid=(nb, pairs),
        in_specs=[blk(0, 0),
                  blk(-1, pairs), blk(0, pairs), blk(1, pairs),
                  blk(-1, 2 * pairs), blk(0, 2 * pairs), blk(1, 2 * pairs),
                  pl.BlockSpec((None,) + bias.shape[1:], lambda i, hp, vp, vn: (hp, 0, 0, 0, 0))],
        out_specs=pl.BlockSpec((b, LANES), lambda i, hp, vp, vn: (i, hp)),
        scratch_shapes=[pltpu.VMEM((g, b // g, LANES), jnp.float32),
                        pltpu.VMEM((g, 3 * b // g, LANES), jnp.float32),
                        pltpu.VMEM((g, 3 * b // g, LANES), jnp.float32),
                        pltpu.VMEM((2, ATT_Q + 2 * ATT_HALF, LANES), jnp.float32),
                        pltpu.VMEM((2, ATT_Q + 2 * ATT_HALF, LANES), jnp.float32),
                        pltpu.VMEM((len(DIL_PATTERNS), 2) + bias.shape[2:], jnp.float32)]
        + [pltpu.VMEM((b, LANES), jnp.float32)] * 3
        + [pltpu.VMEM((2, g, b // g, LANES), jnp.float32)] * 3
        + [pltpu.VMEM((b, LANES), jnp.float32)]
        + [pltpu.VMEM((g, g, b // (g * g), LANES), jnp.float32),
           pltpu.VMEM((g, g, b // (g * g) + 2 * ATT_HALF, LANES), jnp.float32),
           pltpu.VMEM((g, g, b // (g * g) + 2 * ATT_HALF, LANES), jnp.float32)]
        + [pltpu.VMEM((g, g, b // (g * g), LANES), jnp.float32)] * 3)
    return pl.pallas_call(
        _attention_kernel,
        grid_spec=grid_spec,
        out_shape=jax.ShapeDtypeStruct((t, D_MODEL), jnp.bfloat16),
        compiler_params=_cparams("parallel", "arbitrary"),
        name="dilated_attention",
    )(jnp.asarray(vprev), jnp.asarray(vnext), qkv, qkv, qkv, qkv, qkv, qkv, qkv, bias)


PROJ_COLS = 1024


def kernel(x_prompt, x_sample, norm_mix, norm_ffn, norm_final, ret_w_in, ret_w_out, ret_decay_fwd,
           ret_decay_bwd, att_w_qkv, att_w_out, moe_router, moe_w_gate, moe_w_up, moe_w_down):
    d = x_prompt.shape[-1]
    depth = norm_mix.shape[0]
    tp = x_prompt.shape[0] * x_prompt.shape[1]
    ts = x_sample.shape[0] * x_sample.shape[1]
    seq_lens = [x_prompt.shape[1]] * x_prompt.shape[0] + [x_sample.shape[1]] * x_sample.shape[0]
    groups = ((0, tp), (tp, ts))
    bf16 = lambda w: w.astype(jnp.bfloat16)

    x = (x_prompt.reshape(tp, d), x_sample.reshape(ts, d))
    for i in range(depth):
        j = i // 2
        last = i == depth - 1
        if i % 2 == 0:
            proj = norm_matmul(x, norm_mix[i], bf16(ret_w_in[j]), jnp.bfloat16, PROJ_COLS)
            y_fwd = retention(proj, seq_lens, ret_decay_fwd[j], False)
            z = retention(proj, seq_lens, ret_decay_bwd[j], True, y_fwd)
            w_out = ret_w_out[j]
        else:
            qkv = norm_matmul(x, norm_mix[i], bf16(att_w_qkv[j]), jnp.float32, PROJ_COLS)
            z = dilated_attention(qkv, seq_lens)
            w_out = att_w_out[j]
        x1, hn, aff = outproj_router(z, bf16(w_out), x, norm_ffn[i], moe_router[i])
        x = moe_layer(x1, hn, aff, groups, moe_w_gate, moe_w_up, moe_w_down, i,
                      norm_final, final=last, out_rows=(tp, ts) if last else (tp + ts,))
    return x[0].reshape(x_prompt.shape), x[1].reshape(x_sample.shape)
```

```python
import functools
import math

import jax
import jax.numpy as jnp
import numpy as np
from jax import lax
from jax.experimental import pallas as pl
from jax.experimental.pallas import tpu as pltpu

D_MODEL = 1024
RET_HEADS = 4
RET_QK_DIM = D_MODEL // RET_HEADS
RET_V_WIDTH = 2 * D_MODEL
RET_V_DIM = RET_V_WIDTH // RET_HEADS
ATT_HEADS = 16
ATT_HEAD_DIM = D_MODEL // ATT_HEADS
DIL_PATTERNS = ((128, 1), (512, 4), (2048, 16))
N_EXPERTS = 16
EC_CAPACITY_FACTOR = 2
RMS_EPS = 1e-6
NEG_INF = -1e30

LANES = 128
BF16_ROWS = 16
VMEM_LIMIT = 56 * 1024 * 1024

ROW_TILE = 512
MOE_TILE = 256
MOE_WIN = 64
MOE_MAX_PASSES = MOE_TILE // MOE_WIN + 1
FFN_ROWS = 512
FFN_CHUNK = 256


def _cparams(*sem):
    return pltpu.CompilerParams(dimension_semantics=sem, vmem_limit_bytes=VMEM_LIMIT)


def _part_layout(parts, tile):
    specs, firsts, first = [], [], 0
    for p in parts:
        n = p.shape[0] // tile
        specs.append(pl.BlockSpec((tile, p.shape[1]),
                                  lambda i, *_, first=first, n=n: (jnp.clip(i - first, 0, n - 1), 0)))
        firsts.append(first)
        first += n
    return specs, tuple(firsts), first


def _read_parts(i, refs, firsts):
    x = refs[0][...]
    for ref, first in zip(refs[1:], firsts[1:]):
        x = jnp.where(i >= first, ref[...], x)
    return x


def _norm_matmul_kernel(*refs, tn, firsts):
    x_refs, (g_ref, w_ref, o_ref) = refs[:len(firsts)], refs[len(firsts):]
    x = _read_parts(pl.program_id(0), x_refs, firsts)
    ms = jnp.mean(x * x, axis=-1, keepdims=True)
    hn = (x * lax.rsqrt(ms + RMS_EPS) * g_ref[...]).astype(jnp.bfloat16)
    for c in range(w_ref.shape[1] // tn):
        cols = slice(c * tn, (c + 1) * tn)
        o_ref[:, cols] = jnp.dot(hn, w_ref[:, cols],
                                 preferred_element_type=jnp.float32).astype(o_ref.dtype)


def norm_matmul(x_parts, g, w_bf16, out_dtype, tn):
    d, n = w_bf16.shape
    x_specs, firsts, tiles = _part_layout(x_parts, ROW_TILE)
    return pl.pallas_call(
        functools.partial(_norm_matmul_kernel, tn=tn, firsts=firsts),
        grid=(tiles,),
        in_specs=x_specs + [pl.BlockSpec((1, d), lambda i: (0, 0)),
                            pl.BlockSpec((d, n), lambda i: (0, 0), pipeline_mode=pl.Buffered(1))],
        out_specs=pl.BlockSpec((ROW_TILE, n), lambda i: (i, 0)),
        out_shape=jax.ShapeDtypeStruct((tiles * ROW_TILE, n), out_dtype),
        compiler_params=_cparams("parallel"),
        name="norm_matmul",
    )(*x_parts, g.reshape(1, d), w_bf16)


def _outproj_router_kernel(z_ref, w_ref, g_ref, wr_ref, *refs, firsts):
    x_refs, (x1_ref, hn_ref, aff_ref) = refs[:len(firsts)], refs[len(firsts):]
    x = _read_parts(pl.program_id(0), x_refs, firsts)
    x1 = x + jnp.dot(z_ref[...], w_ref[...], preferred_element_type=jnp.float32)
    x1_ref[...] = x1
    ms = jnp.mean(x1 * x1, axis=-1, keepdims=True)
    hn = x1 * lax.rsqrt(ms + RMS_EPS) * g_ref[...]
    hn_hi = hn.astype(jnp.bfloat16)
    hn_ref[...] = hn_hi
    hn_lo = (hn - hn_hi.astype(jnp.float32)).astype(jnp.bfloat16)
    wr = wr_ref[...]
    wr_hi = wr.astype(jnp.bfloat16)
    wr_lo = (wr - wr_hi.astype(jnp.float32)).astype(jnp.bfloat16)
    nt = (((1,), (1,)), ((), ()))
    e = wr.shape[0]
    by_hi = lax.dot_general(jnp.concatenate([wr_hi, wr_lo], axis=0), hn_hi, nt,
                            preferred_element_type=jnp.float32)
    logits = (by_hi[:e] + by_hi[e:]
              + lax.dot_general(wr_hi, hn_lo, nt, preferred_element_type=jnp.float32))
    m = jnp.max(logits, axis=0, keepdims=True)
    p = jnp.exp(logits - m)
    aff_ref[...] = p / jnp.sum(p, axis=0, keepdims=True)


def outproj_router(z, w_bf16, x_parts, g, w_router):
    t, k = z.shape
    d, e = w_router.shape
    x_specs, firsts, tiles = _part_layout(x_parts, ROW_TILE)
    assert tiles * ROW_TILE == t
    return pl.pallas_call(
        functools.partial(_outproj_router_kernel, firsts=firsts),
        grid=(tiles,),
        in_specs=[pl.BlockSpec((ROW_TILE, k), lambda i: (i, 0)),
                  pl.BlockSpec((k, d), lambda i: (0, 0)),
                  pl.BlockSpec((1, d), lambda i: (0, 0)),
                  pl.BlockSpec((e, d), lambda i: (0, 0))] + x_specs,
        out_specs=[pl.BlockSpec((ROW_TILE, d), lambda i: (i, 0)),
                   pl.BlockSpec((ROW_TILE, d), lambda i: (i, 0)),
                   pl.BlockSpec((e, ROW_TILE), lambda i: (0, i))],
        out_shape=[jax.ShapeDtypeStruct((t, d), jnp.float32),
                   jax.ShapeDtypeStruct((t, d), jnp.bfloat16),
                   jax.ShapeDtypeStruct((e, t), jnp.float32)],
        compiler_params=_cparams("parallel"),
        name="outproj_router",
    )(z, w_bf16, g.reshape(1, d), w_router.T, *x_parts)


def _select_kernel(aff_ref, slot_ref, rowstart_ref, *, cap, base):
    e, r, _ = aff_ref.shape
    bits = pltpu.bitcast(aff_ref[...], jnp.int32)

    def count(mask):
        c = jnp.sum(jnp.where(mask, 1.0, 0.0), axis=2, keepdims=True)
        return jnp.sum(c, axis=1, keepdims=True)

    def search(i, thr):
        cand = thr | jnp.left_shift(jnp.int32(1), 30 - i)
        return jnp.where(count(bits >= cand) >= cap, cand, thr)

    thr = lax.fori_loop(0, 31, search, jnp.zeros((e, 1, 1), jnp.int32))
    gt = bits > thr
    eq = bits == thr
    need = cap - count(gt)

    row_i = lax.broadcasted_iota(jnp.int32, (LANES, LANES), 0)
    col_i = lax.broadcasted_iota(jnp.int32, (LANES, LANES), 1)
    upper = jnp.where(row_i <= col_i, 1.0, 0.0).astype(jnp.bfloat16)
    ones = jnp.ones((LANES, LANES), jnp.bfloat16)
    rr = lax.broadcasted_iota(jnp.int32, (r, r), 0)
    rc = lax.broadcasted_iota(jnp.int32, (r, r), 1)
    lower = jnp.where(rc < rr, 1.0, 0.0).astype(jnp.bfloat16)

    def excl_cumsum(mask):
        m = jnp.where(mask, 1.0, 0.0).astype(jnp.bfloat16).reshape(e * r, LANES)
        incl = jnp.dot(m, upper, preferred_element_type=jnp.float32)
        tot = jnp.dot(m, ones, preferred_element_type=jnp.float32)
        offs = []
        for ee in range(e):
            t_e = tot[ee * r:(ee + 1) * r].astype(jnp.bfloat16)
            offs.append(jnp.dot(lower, t_e, preferred_element_type=jnp.float32))
        off = jnp.concatenate(offs, axis=0)
        excl = incl - m.astype(jnp.float32) + off
        return excl.reshape(e, r, LANES), off.reshape(e, r, LANES)

    eq_rank, _ = excl_cumsum(eq)
    sel = gt | (eq & (eq_rank < need))
    pos, off = excl_cumsum(sel)
    slot_ref[...] = jnp.where(sel, pos.astype(jnp.int32) + base, -1)
    rowstart_ref[...] = off.astype(jnp.int32) + base


def select_tokens(aff3, cap, base):
    e, r, _ = aff3.shape
    return pl.pallas_call(
        functools.partial(_select_kernel, cap=cap, base=base),
        out_shape=[jax.ShapeDtypeStruct((e, r, LANES), jnp.int32),
                   jax.ShapeDtypeStruct((e, r, LANES), jnp.int32)],
        compiler_params=pltpu.CompilerParams(vmem_limit_bytes=VMEM_LIMIT),
        name="select_tokens",
    )(aff3)


def _dispatch_kernel(starts_ref, slot_ref, hn_ref, xg_ref, win_ref, carry_ref, sem_ref):
    i = pl.program_id(0)
    nt = pl.num_programs(0)
    e = slot_ref.shape[0]
    tt = slot_ref.shape[1]
    w = MOE_WIN
    buf = i % 2

    def aligned(s):
        return (s // BF16_ROWS) * BF16_ROWS

    a = [aligned(starts_ref[i * e + ee]) for ee in range(e)]
    end = [starts_ref[(i + 1) * e + ee] for ee in range(e)]
    n_pass = jnp.int32(1)
    for ee in range(e):
        n_pass = jnp.maximum(n_pass, (end[ee] - a[ee] + (w - 1)) // w)

    @pl.when(i == 0)
    def _():
        carry_ref[...] = jnp.zeros_like(carry_ref)
        win_ref[1, 0] = jnp.zeros(win_ref.shape[2:], win_ref.dtype)
        tail = [pltpu.make_async_copy(win_ref.at[1, 0], xg_ref.at[ee, pl.ds(r0, w), :], sem_ref.at[1, ee])
                for ee in range(e) for r0 in range(xg_ref.shape[1] - MOE_MAX_PASSES * w, xg_ref.shape[1], w)]
        for c in tail:
            c.start()
        for c in tail:
            c.wait()

    def copies(b, p):
        return [pltpu.make_async_copy(
            win_ref.at[b, ee],
            xg_ref.at[ee, pl.ds(pl.multiple_of(a[ee] + p * w, BF16_ROWS), w), :],
            sem_ref.at[b, ee]) for ee in range(e)]

    def wait_tile(b, src_i):
        for ee in range(e):
            pltpu.make_async_copy(win_ref.at[b, ee], xg_ref.at[ee, pl.ds(0, w), :],
                                  sem_ref.at[b, ee]).wait()

    hn = hn_ref[...]
    row = lax.broadcasted_iota(jnp.int32, (w, tt), 0)

    def one_pass(p, _):
        onehot = []
        for ee in range(e):
            rel = slot_ref[pl.ds(ee, 1), :] - (a[ee] + p * w)
            onehot.append(jnp.where(row == rel, 1.0, 0.0).astype(jnp.bfloat16))
        onehot = jnp.concatenate(onehot, axis=0)
        rows = jnp.dot(onehot, hn, preferred_element_type=jnp.float32)

        for ee in range(e):
            r_e = rows[ee * w:(ee + 1) * w]
            head = r_e[:BF16_ROWS] + jnp.where(p == 0, carry_ref[ee].astype(jnp.float32), 0.0)
            win_ref[buf, ee, pl.ds(0, BF16_ROWS), :] = head.astype(jnp.bfloat16)
            win_ref[buf, ee, pl.ds(BF16_ROWS, w - BF16_ROWS), :] = r_e[BF16_ROWS:].astype(jnp.bfloat16)

        for ee in range(e):
            nxt = aligned(end[ee]) - (a[ee] + p * w)
            held = win_ref[buf, ee, pl.ds(pl.multiple_of(jnp.clip(nxt, 0, w - BF16_ROWS), BF16_ROWS),
                                          BF16_ROWS), :]
            keep = (nxt >= 0) & (nxt < w)
            clear = (p == n_pass - 1) & (nxt >= w)
            carry_ref[ee] = jnp.where(keep, held, jnp.where(clear, jnp.zeros_like(held), carry_ref[ee]))

        @pl.when((p == 0) & (i > 0))
        def _():
            wait_tile(1 - buf, i - 1)

        for c in copies(buf, p):
            c.start()

        @pl.when(p + 1 < n_pass)
        def _():
            for c in copies(buf, p):
                c.wait()
        return 0

    lax.fori_loop(0, n_pass, one_pass, 0)

    @pl.when(i == nt - 1)
    def _():
        wait_tile(buf, i)


def dispatch(starts, slot, hn, rows_padded):
    e, t = slot.shape
    d = hn.shape[1]
    grid_spec = pltpu.PrefetchScalarGridSpec(
        num_scalar_prefetch=1,
        grid=(t // MOE_TILE,),
        in_specs=[pl.BlockSpec((e, MOE_TILE), lambda i, s: (0, i)),
                  pl.BlockSpec((MOE_TILE, d), lambda i, s: (i, 0))],
        out_specs=pl.BlockSpec(memory_space=pl.ANY),
        scratch_shapes=[pltpu.VMEM((2, e, MOE_WIN, d), jnp.bfloat16),
                        pltpu.VMEM((e, BF16_ROWS, d), jnp.bfloat16),
                        pltpu.SemaphoreType.DMA((2, e))])
    return pl.pallas_call(
        _dispatch_kernel,
        grid_spec=grid_spec,
        out_shape=jax.ShapeDtypeStruct((e, rows_padded, d), jnp.bfloat16),
        compiler_params=_cparams("arbitrary"),
        name="moe_dispatch",
    )(starts, slot, hn)


FFN_SLABS = 16


def _ffn_kernel(x_ref, wg_hbm, wu_hbm, wd_hbm, y_ref, wg_ref, wu_ref, wd_ref, sg_ref, su_ref, sd_ref,
                acc_ref, sem_ref, *, layer, steps):
    ee = pl.program_id(0)
    m = pl.program_id(1)
    d, f = x_ref.shape[1], wg_ref.shape[2]
    in_rows, hid_rows = sg_ref.shape[0], sd_ref.shape[0]
    cur = ee % 2

    def slab_copies(expert, c):
        r_in = pl.ds(pl.multiple_of(c * in_rows, BF16_ROWS), in_rows)
        r_hid = pl.ds(pl.multiple_of(c * hid_rows, BF16_ROWS), hid_rows)
        return [pltpu.make_async_copy(wg_hbm.at[layer, expert, r_in, :], sg_ref, sem_ref.at[0]),
                pltpu.make_async_copy(wu_hbm.at[layer, expert, r_in, :], su_ref, sem_ref.at[1]),
                pltpu.make_async_copy(wd_hbm.at[layer, expert, r_hid, :], sd_ref, sem_ref.at[2])]

    def land(copies, half, c):
        for cp in copies:
            cp.wait()
        r_in = pl.ds(pl.multiple_of(c * in_rows, BF16_ROWS), in_rows)
        r_hid = pl.ds(pl.multiple_of(c * hid_rows, BF16_ROWS), hid_rows)
        wg_ref[half, r_in, :] = sg_ref[...].astype(jnp.bfloat16)
        wu_ref[half, r_in, :] = su_ref[...].astype(jnp.bfloat16)
        wd_ref[half, r_hid, :] = sd_ref[...].astype(jnp.bfloat16)

    @pl.when((ee == 0) & (m == 0))
    def _():
        def fetch(c, _):
            copies = slab_copies(0, c)
            for cp in copies:
                cp.start()
            land(copies, 0, c)
            return 0
        lax.fori_loop(0, FFN_SLABS, fetch, 0)

    per_step = -(-FFN_SLABS // steps)
    first = m * per_step
    has_next = ee + 1 < pl.num_programs(0)
    prefetch = has_next & (first < FFN_SLABS)
    src_expert = jnp.minimum(ee + 1, pl.num_programs(0) - 1)
    copies = slab_copies(src_expert, jnp.minimum(first, FFN_SLABS - 1))
    for cp in copies:
        cp.start()

    x = x_ref[...]
    n_chunks = f // FFN_CHUNK
    for c in range(n_chunks):
        if c == n_chunks // 2:
            land(copies, 1 - cur, jnp.where(prefetch, first, FFN_SLABS))
        cols = slice(c * FFN_CHUNK, (c + 1) * FFN_CHUNK)
        g = jnp.dot(x, wg_ref[cur, 0:d, cols], preferred_element_type=jnp.float32)
        u = jnp.dot(x, wu_ref[cur, 0:d, cols], preferred_element_type=jnp.float32)
        h = (g * jax.nn.sigmoid(g) * u).astype(jnp.bfloat16)
        part = jnp.dot(h, wd_ref[cur, cols, :], preferred_element_type=jnp.float32)
        if c == 0:
            acc_ref[...] = part
        else:
            acc_ref[...] += part
    y_ref[...] = acc_ref[...].astype(y_ref.dtype)

    for k in range(1, per_step):
        @pl.when(has_next & (first + k < FFN_SLABS))
        def _():
            more = slab_copies(ee + 1, first + k)
            for cp in more:
                cp.start()
            land(more, 1 - cur, first + k)


def expert_ffn(xg, wg, wu, wd, layer, rows):
    e, _, d = xg.shape
    f = wg.shape[3]
    in_rows, hid_rows = d // FFN_SLABS, f // FFN_SLABS
    assert in_rows % BF16_ROWS == 0 and hid_rows % BF16_ROWS == 0 and f % FFN_CHUNK == 0
    return pl.pallas_call(
        functools.partial(_ffn_kernel, layer=layer, steps=rows // FFN_ROWS),
        grid=(e, rows // FFN_ROWS),
        in_specs=[pl.BlockSpec((None, FFN_ROWS, d), lambda ee, m: (ee, m, 0)),
                  pl.BlockSpec(memory_space=pl.ANY),
                  pl.BlockSpec(memory_space=pl.ANY),
                  pl.BlockSpec(memory_space=pl.ANY)],
        out_specs=pl.BlockSpec((None, FFN_ROWS, d), lambda ee, m: (ee, m, 0)),
        out_shape=jax.ShapeDtypeStruct((e, rows, d), jnp.bfloat16),
        scratch_shapes=[pltpu.VMEM((2, d + in_rows, f), jnp.bfloat16),
                        pltpu.VMEM((2, d + in_rows, f), jnp.bfloat16),
                        pltpu.VMEM((2, f + hid_rows, d), jnp.bfloat16),
                        pltpu.VMEM((in_rows, f), jnp.float32),
                        pltpu.VMEM((in_rows, f), jnp.float32),
                        pltpu.VMEM((hid_rows, d), jnp.float32),
                        pltpu.VMEM((FFN_ROWS, d), jnp.float32),
                        pltpu.SemaphoreType.DMA((3,))],
        compiler_params=_cparams("arbitrary", "arbitrary"),
        name="expert_ffn",
    )(xg, wg, wu, wd)


def _combine_kernel(starts_ref, slot_ref, gate_ref, x_ref, gain_ref, ys_ref, *refs, rows, final, out_tiles):
    o_refs, (buf_ref, sem_ref) = refs[:len(out_tiles)], refs[len(out_tiles):]
    i = pl.program_id(0)
    nt = pl.num_programs(0)
    tt, e = slot_ref.shape
    w = MOE_WIN
    d = x_ref.shape[1]
    b = i % 2

    def aligned(s):
        return (s // BF16_ROWS) * BF16_ROWS

    def window_start(ti, ee, p):
        return jnp.minimum(aligned(starts_ref[ti * e + ee]) + p * w, rows - w)

    def copies(ti, bb, p):
        return [pltpu.make_async_copy(
            ys_ref.at[ee, pl.ds(pl.multiple_of(window_start(ti, ee, p), BF16_ROWS), w), :],
            buf_ref.at[bb, pl.ds(ee * w, w), :],
            sem_ref.at[bb, ee]) for ee in range(e)]

    @pl.when(i == 0)
    def _():
        for c in copies(i, b, 0):
            c.start()

    @pl.when(i + 1 < nt)
    def _():
        for c in copies(i + 1, 1 - b, 0):
            c.start()

    n_pass = jnp.int32(1)
    for ee in range(e):
        n_pass = jnp.maximum(
            n_pass, (starts_ref[(i + 1) * e + ee] - aligned(starts_ref[i * e + ee]) + (w - 1)) // w)

    slot = slot_ref[...]
    gate = gate_ref[...]
    g_hi = gate.astype(jnp.bfloat16)
    g_lo = (gate - g_hi.astype(jnp.float32)).astype(jnp.bfloat16)
    expert = lax.broadcasted_iota(jnp.int32, (1, e), 1)
    spread = jnp.where(lax.broadcasted_iota(jnp.int32, (e, e * w), 1) // w
                       == lax.broadcasted_iota(jnp.int32, (e, e * w), 0), 1.0, 0.0).astype(jnp.bfloat16)
    g_hi_cols = jnp.dot(g_hi, spread, preferred_element_type=jnp.float32)
    g_lo_cols = jnp.dot(g_lo, spread, preferred_element_type=jnp.float32)
    col_in_window = (lax.broadcasted_iota(jnp.int32, (tt, e * w), 1) % w).astype(jnp.float32)

    def one_pass(p, acc):
        @pl.when(p > 0)
        def _():
            for c in copies(i, b, p):
                c.start()

        for c in copies(i, b, p):
            c.wait()

        lo = jnp.zeros((1, e), jnp.int32)
        ws = jnp.zeros((1, e), jnp.int32)
        for ee in range(e):
            lo = jnp.where(expert == ee, aligned(starts_ref[i * e + ee]) + p * w, lo)
            ws = jnp.where(expert == ee, window_start(i, ee, p), ws)
        ok = (slot >= lo) & (slot < lo + w)
        rel = jnp.where(ok, slot - ws, -1).astype(jnp.float32).astype(jnp.bfloat16)
        rel_cols = jnp.dot(rel, spread, preferred_element_type=jnp.float32)
        hit = rel_cols == col_in_window
        q_hi = jnp.where(hit, g_hi_cols, 0.0).astype(jnp.bfloat16)
        q_lo = jnp.where(hit, g_lo_cols, 0.0).astype(jnp.bfloat16)
        ys = buf_ref[b]
        return (acc + jnp.dot(q_hi, ys, preferred_element_type=jnp.float32)
                + jnp.dot(q_lo, ys, preferred_element_type=jnp.float32))

    moe = lax.fori_loop(0, n_pass, one_pass, jnp.zeros((tt, d), jnp.float32))
    x = x_ref[...] + moe
    if final:
        ms = jnp.mean(x * x, axis=-1, keepdims=True)
        x = x * lax.rsqrt(ms + RMS_EPS) * gain_ref[...]
    first = 0
    for o_ref, n in zip(o_refs, out_tiles):
        @pl.when((i >= first) & (i < first + n))
        def _(o_ref=o_ref):
            o_ref[...] = x
        first += n


def combine(starts, slot_t, gate_t, x, gain, ys, final, out_rows):
    t, e = slot_t.shape
    d = x.shape[1]
    rows = ys.shape[1]
    out_tiles = tuple(r // MOE_TILE for r in out_rows)
    assert sum(out_rows) == t
    out_specs, first = [], 0
    for n in out_tiles:
        out_specs.append(pl.BlockSpec((MOE_TILE, d),
                                      lambda i, s, first=first, n=n: (jnp.clip(i - first, 0, n - 1), 0)))
        first += n
    grid_spec = pltpu.PrefetchScalarGridSpec(
        num_scalar_prefetch=1,
        grid=(t // MOE_TILE,),
        in_specs=[pl.BlockSpec((MOE_TILE, e), lambda i, s: (i, 0)),
                  pl.BlockSpec((MOE_TILE, e), lambda i, s: (i, 0)),
                  pl.BlockSpec((MOE_TILE, d), lambda i, s: (i, 0)),
                  pl.BlockSpec((1, d), lambda i, s: (0, 0)),
                  pl.BlockSpec(memory_space=pl.ANY)],
        out_specs=out_specs,
        scratch_shapes=[pltpu.VMEM((2, e * MOE_WIN, d), jnp.bfloat16),
                        pltpu.SemaphoreType.DMA((2, e))])
    return pl.pallas_call(
        functools.partial(_combine_kernel, rows=rows, final=final, out_tiles=out_tiles),
        grid_spec=grid_spec,
        out_shape=[jax.ShapeDtypeStruct((r, d), jnp.float32) for r in out_rows],
        compiler_params=_cparams("arbitrary"),
        name="moe_combine",
    )(starts, slot_t, gate_t, x, gain.reshape(1, d), ys)


def moe_layer(x1, hn, aff, groups, wg, wu, wd, layer, gain, final, out_rows):
    e, t = aff.shape
    slots, rowstarts = [], []
    base = 0
    for (t0, tg) in groups:
        cap = EC_CAPACITY_FACTOR * tg // e
        aff3 = aff[:, t0:t0 + tg].reshape(e, tg // LANES, LANES)
        s, r = select_tokens(aff3, cap, base)
        slots.append(s.reshape(e, tg))
        rowstarts.append(r[:, ::MOE_TILE // LANES, 0])
        base += cap
    rows = base
    slot = jnp.concatenate(slots, axis=1)
    starts = jnp.concatenate(rowstarts + [jnp.full((e, 1), rows, jnp.int32)], axis=1)
    starts = starts.T.reshape(-1)
    xg = dispatch(starts, slot, hn, rows + MOE_MAX_PASSES * MOE_WIN)
    ys = expert_ffn(xg, wg, wu, wd, layer, rows)
    return tuple(combine(starts, slot.T, aff.T, x1, gain, ys, final, out_rows))


RET_CHUNK = 256


def _retention_kernel(reset_ref, q_ref, k_ref, v_ref, intra_ref, qdec_ref, kdec_ref, cdec_ref,
                      *rest, reverse):
    if reverse:
        yf_ref, g_ref, o_ref, state_ref = rest
    else:
        o_ref, state_ref = rest
    i = pl.program_id(0)

    @pl.when(reset_ref[i] == 1)
    def _():
        state_ref[...] = jnp.zeros_like(state_ref)

    dk, dv = RET_QK_DIM, RET_V_DIM
    for h in range(RET_HEADS):
        q = q_ref[:, h * dk:(h + 1) * dk]
        k = k_ref[:, h * dk:(h + 1) * dk]
        v = v_ref[:, h * dv:(h + 1) * dv]
        state = state_ref[h]
        s = lax.dot_general(q, k, (((1,), (1,)), ((), ())), preferred_element_type=jnp.float32)
        inner = (s * intra_ref[h]).astype(jnp.bfloat16)
        qd = (q.astype(jnp.float32) * qdec_ref[h]).astype(jnp.bfloat16)
        y = (jnp.dot(inner, v, preferred_element_type=jnp.float32)
             + jnp.dot(qd, state.astype(jnp.bfloat16), preferred_element_type=jnp.float32))
        kd = (k.astype(jnp.float32) * kdec_ref[h]).astype(jnp.bfloat16)
        state_ref[h] = state * cdec_ref[h] + lax.dot_general(
            kd, v, (((0,), (0,)), ((), ())), preferred_element_type=jnp.float32)
        if reverse:
            y = y + yf_ref[:, h * dv:(h + 1) * dv].astype(jnp.float32)
            y = y * lax.rsqrt(jnp.mean(y * y, axis=-1, keepdims=True) + RMS_EPS)
            g = g_ref[:, h * dv:(h + 1) * dv].astype(jnp.float32)
            y = g * jax.nn.sigmoid(g) * y
        o_ref[:, h * dv:(h + 1) * dv] = y.astype(o_ref.dtype)


def _decay_tables(log_gamma, reverse):
    c = RET_CHUNK
    lg = log_gamma.astype(jnp.float32)[:, None, None]
    pos = jnp.arange(c, dtype=jnp.float32)
    rel = pos[:, None] - pos[None, :]
    scale = RET_QK_DIM ** -0.5
    if reverse:
        intra = jnp.where(rel < 0, jnp.exp(lg * jnp.maximum(-rel, 0.0)[None]), 0.0)
        qdec = jnp.exp(lg * (c - pos)[None, :, None])
        kdec = jnp.exp(lg * pos[None, :, None])
    else:
        intra = jnp.where(rel >= 0, jnp.exp(lg * jnp.maximum(rel, 0.0)[None]), 0.0)
        qdec = jnp.exp(lg * (pos + 1.0)[None, :, None])
        kdec = jnp.exp(lg * (c - 1.0 - pos)[None, :, None])
    return intra * scale, qdec, kdec * scale, jnp.exp(lg * c)


def retention(proj, seq_lens, decay, reverse, y_fwd=None):
    t = proj.shape[0]
    c = RET_CHUNK
    nc = t // c
    bounds = np.cumsum([0] + [s // c for s in seq_lens])
    reset = np.zeros((nc,), np.int32)
    if reverse:
        reset[nc - bounds[1:]] = 1
        chunk = lambda i, r: nc - 1 - i
    else:
        reset[bounds[:-1]] = 1
        chunk = lambda i, r: i
    intra, qdec, kdec, cdec = _decay_tables(-jnp.exp(decay.astype(jnp.float32)), reverse)
    d, vw, h = D_MODEL, RET_V_WIDTH, RET_HEADS
    const = lambda shape: pl.BlockSpec(shape, lambda i, r: (0,) * len(shape))
    in_specs = [pl.BlockSpec((c, d), lambda i, r: (chunk(i, r), 0)),
                pl.BlockSpec((c, d), lambda i, r: (chunk(i, r), 1)),
                pl.BlockSpec((c, vw), lambda i, r: (chunk(i, r), 1)),
                const((h, c, c)), const((h, c, 1)), const((h, c, 1)), const((h, 1, 1))]
    args = [jnp.asarray(reset), proj, proj, proj, intra, qdec, kdec, cdec]
    if reverse:
        in_specs += [pl.BlockSpec((c, vw), lambda i, r: (chunk(i, r), 0)),
                     pl.BlockSpec((c, vw), lambda i, r: (chunk(i, r), 2))]
        args += [y_fwd, proj]
    grid_spec = pltpu.PrefetchScalarGridSpec(
        num_scalar_prefetch=1, grid=(nc,), in_specs=in_specs,
        out_specs=pl.BlockSpec((c, vw), lambda i, r: (chunk(i, r), 0)),
        scratch_shapes=[pltpu.VMEM((h, RET_QK_DIM, RET_V_DIM), jnp.float32)])
    return pl.pallas_call(
        functools.partial(_retention_kernel, reverse=reverse),
        grid_spec=grid_spec,
        out_shape=jax.ShapeDtypeStruct((t, vw), jnp.bfloat16),
        compiler_params=_cparams("arbitrary"),
        name="retention_bwd" if reverse else "retention_fwd",
    )(*args)


ATT_BLOCK = 1024
ATT_HALF = 64
ATT_Q = 128
ATT_DEINT = 4
assert tuple(d for _, d in DIL_PATTERNS) == (1, ATT_DEINT, ATT_DEINT * ATT_DEINT)
assert all(w == 2 * ATT_HALF * d for w, d in DIL_PATTERNS)


def _attention_kernel(vprev_ref, vnext_ref, q_ref, kp_ref, kc_ref, kn_ref, vp_ref, vc_ref, vn_ref,
                      bias_ref, o_ref, q4_ref, k4_ref, v4_ref, kedge_ref, vedge_ref, edge_ref,
                      num1_ref, m1_ref, l1_ref, num4_ref, m4_ref, l4_ref, onat_ref,
                      q16_ref, k16_ref, v16_ref, num16_ref, m16_ref, l16_ref):
    i = pl.program_id(0)
    b = ATT_BLOCK
    g = ATT_DEINT
    bq = b // g
    h = ATT_HALF
    half_lane = ATT_HEAD_DIM
    no_prev = vprev_ref[i] == 0
    no_next = vnext_ref[i] == 0
    scale = ATT_HEAD_DIM ** -0.5 * math.log2(math.e)
    kv_blocks = ((kp_ref, vp_ref), (kc_ref, vc_ref), (kn_ref, vn_ref))

    for c in range(g):
        q4_ref[c] = q_ref[pl.ds(c, bq, stride=g), :]
        for blk, (k_blk, v_blk) in enumerate(kv_blocks):
            k4_ref[c, blk * bq:(blk + 1) * bq] = k_blk[pl.ds(c, bq, stride=g), :]
            v4_ref[c, blk * bq:(blk + 1) * bq] = v_blk[pl.ds(c, bq, stride=g), :]

    nk1 = ATT_Q + 2 * h
    for edge_buf, (prv, cur, nxt) in ((kedge_ref, (kp_ref, kc_ref, kn_ref)),
                                      (vedge_ref, (vp_ref, vc_ref, vn_ref))):
        edge_buf[0, 0:h] = prv[b - h:b]
        edge_buf[0, h:nk1] = cur[0:nk1 - h]
        edge_buf[1, 0:nk1 - h] = cur[b - (nk1 - h):b]
        edge_buf[1, nk1 - h:nk1] = nxt[0:h]

    for br, (_, d) in enumerate(DIL_PATTERNS):
        nq = min(ATT_Q, b // d)
        nk = nq + 2 * h
        col = lax.broadcasted_iota(jnp.int32, (1, nk), 1)
        before = jnp.where((col < h) & no_prev, NEG_INF, 0.0)
        after = jnp.where((col >= nk - h) & no_next, NEG_INF, 0.0)
        if b // d == nq:
            before = before + after
        for hd in range(2):
            edge_ref[br, 0, hd, 0:nq, 0:nk] = bias_ref[br, hd, 0:nq, 0:nk] + before
            edge_ref[br, 1, hd, 0:nq, 0:nk] = bias_ref[br, hd, 0:nq, 0:nk] + after

    def bias_of(br, j, nsub, nq, nk):
        if j == 0:
            return [edge_ref[br, 0, hd, 0:nq, 0:nk] for hd in range(2)]
        if j == nsub - 1:
            return [edge_ref[br, 1, hd, 0:nq, 0:nk] for hd in range(2)]
        return [bias_ref[br, hd, 0:nq, 0:nk] for hd in range(2)]

    def scores(load_q, load_k, bias):
        q = load_q()
        nq = q.shape[0]
        first = lax.broadcasted_iota(jnp.int32, (nq, LANES), 1) < half_lane
        q = q * scale
        q2 = jnp.concatenate([jnp.where(first, q, 0.0), jnp.where(first, 0.0, q)], axis=0)
        s = lax.dot_general(q2.astype(jnp.bfloat16), load_k().astype(jnp.bfloat16),
                            (((1,), (1,)), ((), ())), preferred_element_type=jnp.float32)
        return s + jnp.concatenate(bias(), axis=0)

    def softmax(s):
        m = jnp.max(s, axis=-1, keepdims=True)
        p = jnp.exp2(s - m)
        return p.astype(jnp.bfloat16), m, jnp.sum(p, axis=-1, keepdims=True)

    def values(p, m, l, load_v, store):
        nq = p.shape[0] // 2
        first = lax.broadcasted_iota(jnp.int32, (nq, LANES), 1) < half_lane
        pv = jnp.dot(p, load_v().astype(jnp.bfloat16), preferred_element_type=jnp.float32)
        store(jnp.where(first, pv[:nq], pv[nq:]), jnp.where(first, m[:nq], m[nq:]),
              jnp.where(first, l[:nq], l[nq:]))

    units = []

    def store_to(num_r, m_r, l_r, idx):
        def store(num, m, l):
            num_r[idx] = num
            m_r[idx] = m
            l_r[idx] = l
        return store

    nq, nk, nsub = ATT_Q, nk1, b // ATT_Q
    for j in range(nsub):
        rows = slice(j * nq, (j + 1) * nq)
        keys = slice(j * nq - h, j * nq - h + nk)
        if j == 0:
            load_k, load_v = (lambda: kedge_ref[0]), (lambda: vedge_ref[0])
        elif j == nsub - 1:
            load_k, load_v = (lambda: kedge_ref[1]), (lambda: vedge_ref[1])
        else:
            load_k, load_v = (lambda keys=keys: kc_ref[keys]), (lambda keys=keys: vc_ref[keys])
        units.append(((lambda rows=rows: q_ref[rows]), load_k, load_v,
                      (lambda j=j, a=(nsub, nq, nk): bias_of(0, j, *a)),
                      store_to(num1_ref, m1_ref, l1_ref, rows)))

    nq = min(ATT_Q, bq)
    nk, nsub = nq + 2 * h, bq // nq
    for c in range(g):
        for j in range(nsub):
            rows = slice(j * nq, (j + 1) * nq)
            keys = slice(bq + j * nq - h, bq + j * nq - h + nk)
            units.append(((lambda c=c, rows=rows: q4_ref[c, rows]),
                          (lambda c=c, keys=keys: k4_ref[c, keys]),
                          (lambda c=c, keys=keys: v4_ref[c, keys]),
                          (lambda j=j, a=(nsub, nq, nk): bias_of(1, j, *a)),
                          store_to(num4_ref, m4_ref, l4_ref, (0, c, rows))))

    nq = bq // g
    nk = nq + 2 * h
    for c in range(g):
        for a in range(g):
            q16_ref[c, a] = q4_ref[c, pl.ds(a, nq, stride=g)]
            k16_ref[c, a] = k4_ref[c, pl.ds(a, nk, stride=g)]
            v16_ref[c, a] = v4_ref[c, pl.ds(a, nk, stride=g)]
            units.append(((lambda c=c, a=a: q16_ref[c, a]), (lambda c=c, a=a: k16_ref[c, a]),
                          (lambda c=c, a=a: v16_ref[c, a]),
                          (lambda a_=(1, nq, nk): bias_of(2, 0, *a_)),
                          store_to(num16_ref, m16_ref, l16_ref, (c, a))))

    s_prev = None
    sm_prev = None
    for t in range(len(units) + 2):
        s_new = scores(units[t][0], units[t][1], units[t][3]) if t < len(units) else None
        sm_new = softmax(s_prev) if s_prev is not None else None
        if sm_prev is not None:
            values(*sm_prev, units[t - 2][2], units[t - 2][4])
        s_prev, sm_prev = s_new, sm_new

    for c in range(g):
        for a in range(g):
            rows = pl.ds(a, nq, stride=g)
            num4_ref[1, c, rows] = num16_ref[c, a]
            m4_ref[1, c, rows] = m16_ref[c, a]
            l4_ref[1, c, rows] = l16_ref[c, a]

    for c in range(g):
        rows = pl.ds(c, bq, stride=g)
        ms = [m1_ref[rows], m4_ref[0, c], m4_ref[1, c]]
        ls = [l1_ref[rows], l4_ref[0, c], l4_ref[1, c]]
        nums = [num1_ref[rows], num4_ref[0, c], num4_ref[1, c]]
        m_max = jnp.maximum(jnp.maximum(ms[0], ms[1]), ms[2])
        wts = [jnp.exp2(mm - m_max) for mm in ms]
        den = wts[0] * ls[0] + wts[1] * ls[1] + wts[2] * ls[2]
        num = wts[0] * nums[0] + wts[1] * nums[1] + wts[2] * nums[2]
        onat_ref[rows] = num / den
    o_ref[...] = onat_ref[...].astype(o_ref.dtype)


def _alibi_bias():
    slopes = jnp.exp2(-8.0 * jnp.arange(1, ATT_HEADS + 1, dtype=jnp.float32) / ATT_HEADS)
    qi = jnp.arange(ATT_Q)
    ki = jnp.arange(ATT_Q + 2 * ATT_HALF) - ATT_HALF
    rel = jnp.abs(ki[None, :] - qi[:, None])
    dil = jnp.asarray([d for _, d in DIL_PATTERNS], jnp.float32)
    bias = -slopes[:, None, None, None] * (dil[None, :, None, None] * rel.astype(jnp.float32)[None, None])
    bias = jnp.where((rel <= ATT_HALF)[None, None], bias * math.log2(math.e), NEG_INF)
    return bias.reshape(ATT_HEADS // 2, 2, len(DIL_PATTERNS), *rel.shape).transpose(0, 2, 1, 3, 4)


def dilated_attention(qkv, seq_lens):
    t = qkv.shape[0]
    b = ATT_BLOCK
    g = ATT_DEINT
    nb = t // b
    bounds = np.cumsum([0] + [s // b for s in seq_lens])
    vprev = np.ones((nb,), np.int32)
    vnext = np.ones((nb,), np.int32)
    vprev[bounds[:-1]] = 0
    vnext[bounds[1:] - 1] = 0
    pairs = ATT_HEADS // 2
    blk = lambda which, off: pl.BlockSpec(
        (b, LANES), lambda i, hp, vp, vn: (jnp.clip(i + which, 0, nb - 1), off + hp))
    bias = _alibi_bias()
    grid_spec = pltpu.PrefetchScalarGridSpec(
        num_scalar_prefetch=2, grid=(nb, pairs),
        in_specs=[blk(0, 0),
                  blk(-1, pairs), blk(0, pairs), blk(1, pairs),
                  blk(-1, 2 * pairs), blk(0, 2 * pairs), blk(1, 2 * pairs),
                  pl.BlockSpec((None,) + bias.shape[1:], lambda i, hp, vp, vn: (hp, 0, 0, 0, 0))],
        out_specs=pl.BlockSpec((b, LANES), lambda i, hp, vp, vn: (i, hp)),
        scratch_shapes=[pltpu.VMEM((g, b // g, LANES), jnp.float32),
                        pltpu.VMEM((g, 3 * b // g, LANES), jnp.float32),
                        pltpu.VMEM((g, 3 * b // g, LANES), jnp.float32),
                        pltpu.VMEM((2, ATT_Q + 2 * ATT_HALF, LANES), jnp.float32),
                        pltpu.VMEM((2, ATT_Q + 2 * ATT_HALF, LANES), jnp.float32),
                        pltpu.VMEM((len(DIL_PATTERNS), 2) + bias.shape[2:], jnp.float32)]
        + [pltpu.VMEM((b, LANES), jnp.float32)] * 3
        + [pltpu.VMEM((2, g, b // g, LANES), jnp.float32)] * 3
        + [pltpu.VMEM((b, LANES), jnp.float32)]
        + [pltpu.VMEM((g, g, b // (g * g), LANES), jnp.float32),
           pltpu.VMEM((g, g, b // (g * g) + 2 * ATT_HALF, LANES), jnp.float32),
           pltpu.VMEM((g, g, b // (g * g) + 2 * ATT_HALF, LANES), jnp.float32)]
        + [pltpu.VMEM((g, g, b // (g * g), LANES), jnp.float32)] * 3)
    return pl.pallas_call(
        _attention_kernel,
        grid_spec=grid_spec,
        out_shape=jax.ShapeDtypeStruct((t, D_MODEL), jnp.bfloat16),
        compiler_params=_cparams("parallel", "arbitrary"),
        name="dilated_attention",
    )(jnp.asarray(vprev), jnp.asarray(vnext), qkv, qkv, qkv, qkv, qkv, qkv, qkv, bias)


PROJ_COLS = 1024


def kernel(x_prompt, x_sample, norm_mix, norm_ffn, norm_final, ret_w_in, ret_w_out, ret_decay_fwd,
           ret_decay_bwd, att_w_qkv, att_w_out, moe_router, moe_w_gate, moe_w_up, moe_w_down):
    d = x_prompt.shape[-1]
    depth = norm_mix.shape[0]
    tp = x_prompt.shape[0] * x_prompt.shape[1]
    ts = x_sample.shape[0] * x_sample.shape[1]
    seq_lens = [x_prompt.shape[1]] * x_prompt.shape[0] + [x_sample.shape[1]] * x_sample.shape[0]
    groups = ((0, tp), (tp, ts))
    bf16 = lambda w: w.astype(jnp.bfloat16)

    x = (x_prompt.reshape(tp, d), x_sample.reshape(ts, d))
    for i in range(depth):
        j = i // 2
        last = i == depth - 1
        if i % 2 == 0:
            proj = norm_matmul(x, norm_mix[i], bf16(ret_w_in[j]), jnp.bfloat16, PROJ_COLS)
            y_fwd = retention(proj, seq_lens, ret_decay_fwd[j], False)
            z = retention(proj, seq_lens, ret_decay_bwd[j], True, y_fwd)
            w_out = ret_w_out[j]
        else:
            qkv = norm_matmul(x, norm_mix[i], bf16(att_w_qkv[j]), jnp.float32, PROJ_COLS)
            z = dilated_attention(qkv, seq_lens)
            w_out = att_w_out[j]
        x1, hn, aff = outproj_router(z, bf16(w_out), x, norm_ffn[i], moe_router[i])
        x = moe_layer(x1, hn, aff, groups, moe_w_gate, moe_w_up, moe_w_down, i,
                      norm_final, final=last, out_rows=(tp, ts) if last else (tp + ts,))
    return x[0].reshape(x_prompt.shape), x[1].reshape(x_sample.shape)
```

```python
import functools
import math

import jax
import jax.numpy as jnp
import numpy as np
from jax import lax
from jax.experimental import pallas as pl
from jax.experimental.pallas import tpu as pltpu

D_MODEL = 1024
RET_HEADS = 4
RET_QK_DIM = D_MODEL // RET_HEADS
RET_V_WIDTH = 2 * D_MODEL
RET_V_DIM = RET_V_WIDTH // RET_HEADS
ATT_HEADS = 16
ATT_HEAD_DIM = D_MODEL // ATT_HEADS
DIL_PATTERNS = ((128, 1), (512, 4), (2048, 16))
N_EXPERTS = 16
EC_CAPACITY_FACTOR = 2
RMS_EPS = 1e-6
NEG_INF = -1e30

LANES = 128
BF16_ROWS = 16
VMEM_LIMIT = 56 * 1024 * 1024

ROW_TILE = 512
MOE_TILE = 256
MOE_WIN = 64
MOE_MAX_PASSES = MOE_TILE // MOE_WIN + 1
FFN_ROWS = 512
FFN_CHUNK = 256


def _cparams(*sem):
    return pltpu.CompilerParams(dimension_semantics=sem, vmem_limit_bytes=VMEM_LIMIT)


def _part_layout(parts, tile):
    specs, firsts, first = [], [], 0
    for p in parts:
        n = p.shape[0] // tile
        specs.append(pl.BlockSpec((tile, p.shape[1]),
                                  lambda i, *_, first=first, n=n: (jnp.clip(i - first, 0, n - 1), 0)))
        firsts.append(first)
        first += n
    return specs, tuple(firsts), first


def _read_parts(i, refs, firsts):
    x = refs[0][...]
    for ref, first in zip(refs[1:], firsts[1:]):
        x = jnp.where(i >= first, ref[...], x)
    return x


def _norm_matmul_kernel(*refs, tn, firsts):
    x_refs, (g_ref, w_ref, o_ref) = refs[:len(firsts)], refs[len(firsts):]
    x = _read_parts(pl.program_id(0), x_refs, firsts)
    ms = jnp.mean(x * x, axis=-1, keepdims=True)
    hn = (x * lax.rsqrt(ms + RMS_EPS) * g_ref[...]).astype(jnp.bfloat16)
    for c in range(w_ref.shape[1] // tn):
        cols = slice(c * tn, (c + 1) * tn)
        o_ref[:, cols] = jnp.dot(hn, w_ref[:, cols],
                                 preferred_element_type=jnp.float32).astype(o_ref.dtype)


def norm_matmul(x_parts, g, w_bf16, out_dtype, tn):
    d, n = w_bf16.shape
    x_specs, firsts, tiles = _part_layout(x_parts, ROW_TILE)
    return pl.pallas_call(
        functools.partial(_norm_matmul_kernel, tn=tn, firsts=firsts),
        grid=(tiles,),
        in_specs=x_specs + [pl.BlockSpec((1, d), lambda i: (0, 0)),
                            pl.BlockSpec((d, n), lambda i: (0, 0), pipeline_mode=pl.Buffered(1))],
        out_specs=pl.BlockSpec((ROW_TILE, n), lambda i: (i, 0)),
        out_shape=jax.ShapeDtypeStruct((tiles * ROW_TILE, n), out_dtype),
        compiler_params=_cparams("parallel"),
        name="norm_matmul",
    )(*x_parts, g.reshape(1, d), w_bf16)


def _outproj_router_kernel(z_ref, w_ref, g_ref, wr_ref, *refs, firsts):
    x_refs, (x1_ref, hn_ref, aff_ref) = refs[:len(firsts)], refs[len(firsts):]
    x = _read_parts(pl.program_id(0), x_refs, firsts)
    x1 = x + jnp.dot(z_ref[...], w_ref[...], preferred_element_type=jnp.float32)
    x1_ref[...] = x1
    ms = jnp.mean(x1 * x1, axis=-1, keepdims=True)
    hn = x1 * lax.rsqrt(ms + RMS_EPS) * g_ref[...]
    hn_hi = hn.astype(jnp.bfloat16)
    hn_ref[...] = hn_hi
    hn_lo = (hn - hn_hi.astype(jnp.float32)).astype(jnp.bfloat16)
    wr = wr_ref[...]
    wr_hi = wr.astype(jnp.bfloat16)
    wr_lo = (wr - wr_hi.astype(jnp.float32)).astype(jnp.bfloat16)
    nt = (((1,), (1,)), ((), ()))
    e = wr.shape[0]
    by_hi = lax.dot_general(jnp.concatenate([wr_hi, wr_lo], axis=0), hn_hi, nt,
                            preferred_element_type=jnp.float32)
    logits = (by_hi[:e] + by_hi[e:]
              + lax.dot_general(wr_hi, hn_lo, nt, preferred_element_type=jnp.float32))
    m = jnp.max(logits, axis=0, keepdims=True)
    p = jnp.exp(logits - m)
    aff_ref[...] = p / jnp.sum(p, axis=0, keepdims=True)


def outproj_router(z, w_bf16, x_parts, g, w_router):
    t, k = z.shape
    d, e = w_router.shape
    x_specs, firsts, tiles = _part_layout(x_parts, ROW_TILE)
    assert tiles * ROW_TILE == t
    return pl.pallas_call(
        functools.partial(_outproj_router_kernel, firsts=firsts),
        grid=(tiles,),
        in_specs=[pl.BlockSpec((ROW_TILE, k), lambda i: (i, 0)),
                  pl.BlockSpec((k, d), lambda i: (0, 0)),
                  pl.BlockSpec((1, d), lambda i: (0, 0)),
                  pl.BlockSpec((e, d), lambda i: (0, 0))] + x_specs,
        out_specs=[pl.BlockSpec((ROW_TILE, d), lambda i: (i, 0)),
                   pl.BlockSpec((ROW_TILE, d), lambda i: (i, 0)),
                   pl.BlockSpec((e, ROW_TILE), lambda i: (0, i))],
        out_shape=[jax.ShapeDtypeStruct((t, d), jnp.float32),
                   jax.ShapeDtypeStruct((t, d), jnp.bfloat16),
                   jax.ShapeDtypeStruct((e, t), jnp.float32)],
        compiler_params=_cparams("parallel"),
        name="outproj_router",
    )(z, w_bf16, g.reshape(1, d), w_router.T, *x_parts)


def _select_kernel(aff_ref, slot_ref, rowstart_ref, *, cap, base):
    e, r, _ = aff_ref.shape
    bits = pltpu.bitcast(aff_ref[...], jnp.int32)

    def count(mask):
        c = jnp.sum(jnp.where(mask, 1.0, 0.0), axis=2, keepdims=True)
        return jnp.sum(c, axis=1, keepdims=True)

    def search(i, thr):
        cand = thr | jnp.left_shift(jnp.int32(1), 30 - i)
        return jnp.where(count(bits >= cand) >= cap, cand, thr)

    thr = lax.fori_loop(0, 31, search, jnp.zeros((e, 1, 1), jnp.int32))
    gt = bits > thr
    eq = bits == thr
    need = cap - count(gt)

    row_i = lax.broadcasted_iota(jnp.int32, (LANES, LANES), 0)
    col_i = lax.broadcasted_iota(jnp.int32, (LANES, LANES), 1)
    upper = jnp.where(row_i <= col_i, 1.0, 0.0).astype(jnp.bfloat16)
    ones = jnp.ones((LANES, LANES), jnp.bfloat16)
    rr = lax.broadcasted_iota(jnp.int32, (r, r), 0)
    rc = lax.broadcasted_iota(jnp.int32, (r, r), 1)
    lower = jnp.where(rc < rr, 1.0, 0.0).astype(jnp.bfloat16)

    def excl_cumsum(mask):
        m = jnp.where(mask, 1.0, 0.0).astype(jnp.bfloat16).reshape(e * r, LANES)
        incl = jnp.dot(m, upper, preferred_element_type=jnp.float32)
        tot = jnp.dot(m, ones, preferred_element_type=jnp.float32)
        offs = []
        for ee in range(e):
            t_e = tot[ee * r:(ee + 1) * r].astype(jnp.bfloat16)
            offs.append(jnp.dot(lower, t_e, preferred_element_type=jnp.float32))
        off = jnp.concatenate(offs, axis=0)
        excl = incl - m.astype(jnp.float32) + off
        return excl.reshape(e, r, LANES), off.reshape(e, r, LANES)

    eq_rank, _ = excl_cumsum(eq)
    sel = gt | (eq & (eq_rank < need))
    pos, off = excl_cumsum(sel)
    slot_ref[...] = jnp.where(sel, pos.astype(jnp.int32) + base, -1)
    rowstart_ref[...] = off.astype(jnp.int32) + base


def select_tokens(aff3, cap, base):
    e, r, _ = aff3.shape
    return pl.pallas_call(
        functools.partial(_select_kernel, cap=cap, base=base),
        out_shape=[jax.ShapeDtypeStruct((e, r, LANES), jnp.int32),
                   jax.ShapeDtypeStruct((e, r, LANES), jnp.int32)],
        compiler_params=pltpu.CompilerParams(vmem_limit_bytes=VMEM_LIMIT),
        name="select_tokens",
    )(aff3)


def _dispatch_kernel(starts_ref, slot_ref, hn_ref, gp_ref, xg_ref, win_ref, carry_ref, sem_ref):
    i = pl.program_id(0)
    nt = pl.num_programs(0)
    e = slot_ref.shape[0]
    tt = slot_ref.shape[1]
    w = MOE_WIN
    buf = i % 2

    def aligned(s):
        return (s // BF16_ROWS) * BF16_ROWS

    a = [aligned(starts_ref[i * e + ee]) for ee in range(e)]
    end = [starts_ref[(i + 1) * e + ee] for ee in range(e)]
    n_pass = jnp.int32(1)
    for ee in range(e):
        n_pass = jnp.maximum(n_pass, (end[ee] - a[ee] + (w - 1)) // w)

    @pl.when(i == 0)
    def _():
        carry_ref[...] = jnp.zeros_like(carry_ref)
        win_ref[1, 0] = jnp.zeros(win_ref.shape[2:], win_ref.dtype)
        tail = [pltpu.make_async_copy(win_ref.at[1, 0], xg_ref.at[ee, pl.ds(r0, w), :], sem_ref.at[1, ee])
                for ee in range(e) for r0 in range(xg_ref.shape[1] - MOE_MAX_PASSES * w, xg_ref.shape[1], w)]
        for c in tail:
            c.start()
        for c in tail:
            c.wait()

    def copies(b, p):
        return [pltpu.make_async_copy(
            win_ref.at[b, ee],
            xg_ref.at[ee, pl.ds(pl.multiple_of(a[ee] + p * w, BF16_ROWS), w), :],
            sem_ref.at[b, ee]) for ee in range(e)]

    def wait_tile(b, src_i):
        for ee in range(e):
            pltpu.make_async_copy(win_ref.at[b, ee], xg_ref.at[ee, pl.ds(0, w), :],
                                  sem_ref.at[b, ee]).wait()

    hn = hn_ref[...]
    row = lax.broadcasted_iota(jnp.int32, (w, tt), 0)

    def one_pass(p, _):
        onehot = []
        for ee in range(e):
            rel = slot_ref[pl.ds(ee, 1), :] - (a[ee] + p * w)
            onehot.append(jnp.where(row == rel, 1.0, 0.0).astype(jnp.bfloat16))
        onehot = jnp.concatenate(onehot, axis=0)
        rows = jnp.concatenate([jnp.dot(onehot, hn, preferred_element_type=jnp.float32),
                                jnp.dot(onehot, gp_ref[...], preferred_element_type=jnp.float32)], axis=1)

        for ee in range(e):
            r_e = rows[ee * w:(ee + 1) * w]
            head = r_e[:BF16_ROWS] + jnp.where(p == 0, carry_ref[ee].astype(jnp.float32), 0.0)
            win_ref[buf, ee, pl.ds(0, BF16_ROWS), :] = head.astype(jnp.bfloat16)
            win_ref[buf, ee, pl.ds(BF16_ROWS, w - BF16_ROWS), :] = r_e[BF16_ROWS:].astype(jnp.bfloat16)

        for ee in range(e):
            nxt = aligned(end[ee]) - (a[ee] + p * w)
            held = win_ref[buf, ee, pl.ds(pl.multiple_of(jnp.clip(nxt, 0, w - BF16_ROWS), BF16_ROWS),
                                          BF16_ROWS), :]
            keep = (nxt >= 0) & (nxt < w)
            clear = (p == n_pass - 1) & (nxt >= w)
            carry_ref[ee] = jnp.where(keep, held, jnp.where(clear, jnp.zeros_like(held), carry_ref[ee]))

        @pl.when((p == 0) & (i > 0))
        def _():
            wait_tile(1 - buf, i - 1)

        for c in copies(buf, p):
            c.start()

        @pl.when(p + 1 < n_pass)
        def _():
            for c in copies(buf, p):
                c.wait()
        return 0

    lax.fori_loop(0, n_pass, one_pass, 0)

    @pl.when(i == nt - 1)
    def _():
        wait_tile(buf, i)


def dispatch(starts, slot, hn, gate_pieces, rows_padded):
    e, t = slot.shape
    d = hn.shape[1] + gate_pieces.shape[1]
    grid_spec = pltpu.PrefetchScalarGridSpec(
        num_scalar_prefetch=1,
        grid=(t // MOE_TILE,),
        in_specs=[pl.BlockSpec((e, MOE_TILE), lambda i, s: (0, i)),
                  pl.BlockSpec((MOE_TILE, hn.shape[1]), lambda i, s: (i, 0)),
                  pl.BlockSpec((MOE_TILE, gate_pieces.shape[1]), lambda i, s: (i, 0))],
        out_specs=pl.BlockSpec(memory_space=pl.ANY),
        scratch_shapes=[pltpu.VMEM((2, e, MOE_WIN, d), jnp.bfloat16),
                        pltpu.VMEM((e, BF16_ROWS, d), jnp.bfloat16),
                        pltpu.SemaphoreType.DMA((2, e))])
    return pl.pallas_call(
        _dispatch_kernel,
        grid_spec=grid_spec,
        out_shape=jax.ShapeDtypeStruct((e, rows_padded, d), jnp.bfloat16),
        compiler_params=_cparams("arbitrary"),
        name="moe_dispatch",
    )(starts, slot, hn, gate_pieces)


FFN_SLABS = 16


def _ffn_kernel(x_ref, wg_hbm, wu_hbm, wd_hbm, y_ref, wg_ref, wu_ref, wd_ref, sg_ref, su_ref, sd_ref,
                acc_ref, sem_ref, *, layer, steps):
    ee = pl.program_id(0)
    m = pl.program_id(1)
    d, f = acc_ref.shape[1], wg_ref.shape[2]
    in_rows, hid_rows = sg_ref.shape[0], sd_ref.shape[0]
    cur = ee % 2

    def slab_copies(expert, c):
        r_in = pl.ds(pl.multiple_of(c * in_rows, BF16_ROWS), in_rows)
        r_hid = pl.ds(pl.multiple_of(c * hid_rows, BF16_ROWS), hid_rows)
        return [pltpu.make_async_copy(wg_hbm.at[layer, expert, r_in, :], sg_ref, sem_ref.at[0]),
                pltpu.make_async_copy(wu_hbm.at[layer, expert, r_in, :], su_ref, sem_ref.at[1]),
                pltpu.make_async_copy(wd_hbm.at[layer, expert, r_hid, :], sd_ref, sem_ref.at[2])]

    def land(copies, half, c):
        for cp in copies:
            cp.wait()
        r_in = pl.ds(pl.multiple_of(c * in_rows, BF16_ROWS), in_rows)
        r_hid = pl.ds(pl.multiple_of(c * hid_rows, BF16_ROWS), hid_rows)
        wg_ref[half, r_in, :] = sg_ref[...].astype(jnp.bfloat16)
        wu_ref[half, r_in, :] = su_ref[...].astype(jnp.bfloat16)
        wd_ref[half, r_hid, :] = sd_ref[...].astype(jnp.bfloat16)

    @pl.when((ee == 0) & (m == 0))
    def _():
        def fetch(c, _):
            copies = slab_copies(0, c)
            for cp in copies:
                cp.start()
            land(copies, 0, c)
            return 0
        lax.fori_loop(0, FFN_SLABS, fetch, 0)

    per_step = -(-FFN_SLABS // steps)
    first = m * per_step
    has_next = ee + 1 < pl.num_programs(0)
    prefetch = has_next & (first < FFN_SLABS)
    src_expert = jnp.minimum(ee + 1, pl.num_programs(0) - 1)
    copies = slab_copies(src_expert, jnp.minimum(first, FFN_SLABS - 1))
    for cp in copies:
        cp.start()

    x = x_ref[:, 0:d]
    n_chunks = f // FFN_CHUNK
    for c in range(n_chunks):
        if c == n_chunks // 2:
            land(copies, 1 - cur, jnp.where(prefetch, first, FFN_SLABS))
        cols = slice(c * FFN_CHUNK, (c + 1) * FFN_CHUNK)
        g = jnp.dot(x, wg_ref[cur, 0:d, cols], preferred_element_type=jnp.float32)
        u = jnp.dot(x, wu_ref[cur, 0:d, cols], preferred_element_type=jnp.float32)
        h = (g * jax.nn.sigmoid(g) * u).astype(jnp.bfloat16)
        part = jnp.dot(h, wd_ref[cur, cols, :], preferred_element_type=jnp.float32)
        if c == 0:
            acc_ref[...] = part
        else:
            acc_ref[...] += part
    pieces = x_ref[:, d:].astype(jnp.float32)
    lane = lax.broadcasted_iota(jnp.int32, pieces.shape, 1)
    n_exp = pl.num_programs(0)
    mine = (lane == ee) | (lane == n_exp + ee) | (lane == 2 * n_exp + ee)
    gate = jnp.sum(jnp.where(mine, pieces, 0.0), axis=1, keepdims=True)
    y_ref[...] = (acc_ref[...] * gate).astype(y_ref.dtype)

    for k in range(1, per_step):
        @pl.when(has_next & (first + k < FFN_SLABS))
        def _():
            more = slab_copies(ee + 1, first + k)
            for cp in more:
                cp.start()
            land(more, 1 - cur, first + k)


def expert_ffn(xg, wg, wu, wd, layer, rows):
    e, _, dx = xg.shape
    d, f = wg.shape[2], wg.shape[3]
    in_rows, hid_rows = d // FFN_SLABS, f // FFN_SLABS
    assert in_rows % BF16_ROWS == 0 and hid_rows % BF16_ROWS == 0 and f % FFN_CHUNK == 0
    return pl.pallas_call(
        functools.partial(_ffn_kernel, layer=layer, steps=rows // FFN_ROWS),
        grid=(e, rows // FFN_ROWS),
        in_specs=[pl.BlockSpec((None, FFN_ROWS, dx), lambda ee, m: (ee, m, 0)),
                  pl.BlockSpec(memory_space=pl.ANY),
                  pl.BlockSpec(memory_space=pl.ANY),
                  pl.BlockSpec(memory_space=pl.ANY)],
        out_specs=pl.BlockSpec((None, FFN_ROWS, d), lambda ee, m: (ee, m, 0)),
        out_shape=jax.ShapeDtypeStruct((e, rows, d), jnp.bfloat16),
        scratch_shapes=[pltpu.VMEM((2, d + in_rows, f), jnp.bfloat16),
                        pltpu.VMEM((2, d + in_rows, f), jnp.bfloat16),
                        pltpu.VMEM((2, f + hid_rows, d), jnp.bfloat16),
                        pltpu.VMEM((in_rows, f), jnp.float32),
                        pltpu.VMEM((in_rows, f), jnp.float32),
                        pltpu.VMEM((hid_rows, d), jnp.float32),
                        pltpu.VMEM((FFN_ROWS, d), jnp.float32),
                        pltpu.SemaphoreType.DMA((3,))],
        compiler_params=_cparams("arbitrary", "arbitrary"),
        name="expert_ffn",
    )(xg, wg, wu, wd)


def _combine_kernel(starts_ref, slot_ref, x_ref, gain_ref, ys_ref, *refs, rows, final, out_tiles):
    o_refs, (buf_ref, sem_ref) = refs[:len(out_tiles)], refs[len(out_tiles):]
    i = pl.program_id(0)
    nt = pl.num_programs(0)
    tt, e = slot_ref.shape
    w = MOE_WIN
    d = x_ref.shape[1]
    b = i % 2

    def aligned(s):
        return (s // BF16_ROWS) * BF16_ROWS

    def window_start(ti, ee, p):
        return jnp.minimum(aligned(starts_ref[ti * e + ee]) + p * w, rows - w)

    def copies(ti, bb, p):
        return [pltpu.make_async_copy(
            ys_ref.at[ee, pl.ds(pl.multiple_of(window_start(ti, ee, p), BF16_ROWS), w), :],
            buf_ref.at[bb, pl.ds(ee * w, w), :],
            sem_ref.at[bb, ee]) for ee in range(e)]

    @pl.when(i == 0)
    def _():
        for c in copies(i, b, 0):
            c.start()

    @pl.when(i + 1 < nt)
    def _():
        for c in copies(i + 1, 1 - b, 0):
            c.start()

    n_pass = jnp.int32(1)
    for ee in range(e):
        n_pass = jnp.maximum(
            n_pass, (starts_ref[(i + 1) * e + ee] - aligned(starts_ref[i * e + ee]) + (w - 1)) // w)

    slot = slot_ref[...]
    expert = lax.broadcasted_iota(jnp.int32, (1, e), 1)
    spread = jnp.where(lax.broadcasted_iota(jnp.int32, (e, e * w), 1) // w
                       == lax.broadcasted_iota(jnp.int32, (e, e * w), 0), 1.0, 0.0).astype(jnp.bfloat16)
    col_in_window = (lax.broadcasted_iota(jnp.int32, (tt, e * w), 1) % w).astype(jnp.float32)

    def one_pass(p, acc):
        @pl.when(p > 0)
        def _():
            for c in copies(i, b, p):
                c.start()

        for c in copies(i, b, p):
            c.wait()

        lo = jnp.zeros((1, e), jnp.int32)
        ws = jnp.zeros((1, e), jnp.int32)
        for ee in range(e):
            lo = jnp.where(expert == ee, aligned(starts_ref[i * e + ee]) + p * w, lo)
            ws = jnp.where(expert == ee, window_start(i, ee, p), ws)
        ok = (slot >= lo) & (slot < lo + w)
        rel = jnp.where(ok, slot - ws, -1).astype(jnp.float32).astype(jnp.bfloat16)
        rel_cols = jnp.dot(rel, spread, preferred_element_type=jnp.float32)
        onehot = jnp.where(rel_cols == col_in_window, 1.0, 0.0).astype(jnp.bfloat16)
        return acc + jnp.dot(onehot, buf_ref[b], preferred_element_type=jnp.float32)

    moe = lax.fori_loop(0, n_pass, one_pass, jnp.zeros((tt, d), jnp.float32))
    x = x_ref[...] + moe
    if final:
        ms = jnp.mean(x * x, axis=-1, keepdims=True)
        x = x * lax.rsqrt(ms + RMS_EPS) * gain_ref[...]
    first = 0
    for o_ref, n in zip(o_refs, out_tiles):
        @pl.when((i >= first) & (i < first + n))
        def _(o_ref=o_ref):
            o_ref[...] = x
        first += n


def combine(starts, slot_t, x, gain, ys, final, out_rows):
    t, e = slot_t.shape
    d = x.shape[1]
    rows = ys.shape[1]
    out_tiles = tuple(r // MOE_TILE for r in out_rows)
    assert sum(out_rows) == t
    out_specs, first = [], 0
    for n in out_tiles:
        out_specs.append(pl.BlockSpec((MOE_TILE, d),
                                      lambda i, s, first=first, n=n: (jnp.clip(i - first, 0, n - 1), 0)))
        first += n
    grid_spec = pltpu.PrefetchScalarGridSpec(
        num_scalar_prefetch=1,
        grid=(t // MOE_TILE,),
        in_specs=[pl.BlockSpec((MOE_TILE, e), lambda i, s: (i, 0)),
                  pl.BlockSpec((MOE_TILE, d), lambda i, s: (i, 0)),
                  pl.BlockSpec((1, d), lambda i, s: (0, 0)),
                  pl.BlockSpec(memory_space=pl.ANY)],
        out_specs=out_specs,
        scratch_shapes=[pltpu.VMEM((2, e * MOE_WIN, d), jnp.bfloat16),
                        pltpu.SemaphoreType.DMA((2, e))])
    return pl.pallas_call(
        functools.partial(_combine_kernel, rows=rows, final=final, out_tiles=out_tiles),
        grid_spec=grid_spec,
        out_shape=[jax.ShapeDtypeStruct((r, d), jnp.float32) for r in out_rows],
        compiler_params=_cparams("arbitrary"),
        name="moe_combine",
    )(starts, slot_t, x, gain.reshape(1, d), ys)


def moe_layer(x1, hn, aff, groups, wg, wu, wd, layer, gain, final, out_rows):
    e, t = aff.shape
    slots, rowstarts = [], []
    base = 0
    for (t0, tg) in groups:
        cap = EC_CAPACITY_FACTOR * tg // e
        aff3 = aff[:, t0:t0 + tg].reshape(e, tg // LANES, LANES)
        s, r = select_tokens(aff3, cap, base)
        slots.append(s.reshape(e, tg))
        rowstarts.append(r[:, ::MOE_TILE // LANES, 0])
        base += cap
    rows = base
    slot = jnp.concatenate(slots, axis=1)
    starts = jnp.concatenate(rowstarts + [jnp.full((e, 1), rows, jnp.int32)], axis=1)
    starts = starts.T.reshape(-1)
    gate = aff.T
    g_hi = gate.astype(jnp.bfloat16)
    g_mid = (gate - g_hi.astype(jnp.float32)).astype(jnp.bfloat16)
    g_lo = (gate - g_hi.astype(jnp.float32) - g_mid.astype(jnp.float32)).astype(jnp.bfloat16)
    pieces = jnp.concatenate([g_hi, g_mid, g_lo, jnp.zeros((t, LANES - 3 * e), jnp.bfloat16)], axis=1)
    xg = dispatch(starts, slot, hn, pieces, rows + MOE_MAX_PASSES * MOE_WIN)
    ys = expert_ffn(xg, wg, wu, wd, layer, rows)
    return tuple(combine(starts, slot.T, x1, gain, ys, final, out_rows))


RET_CHUNK = 256


def _retention_kernel(reset_ref, q_ref, k_ref, v_ref, qdec_ref, kdec_ref, cdec_ref, *rest, reverse):
    if reverse:
        yf_ref, g_ref, o_ref, state_ref = rest
    else:
        intra_ref, o_ref, state_ref = rest
    i = pl.program_id(0)

    @pl.when(reset_ref[i] == 1)
    def _():
        state_ref[...] = jnp.zeros_like(state_ref)

    dk, dv = RET_QK_DIM, RET_V_DIM
    for h in range(RET_HEADS):
        q = q_ref[:, h * dk:(h + 1) * dk]
        k = k_ref[:, h * dk:(h + 1) * dk]
        v = v_ref[:, h * dv:(h + 1) * dv]
        state = state_ref[h]
        if not reverse:
            s = lax.dot_general(q, k, (((1,), (1,)), ((), ())), preferred_element_type=jnp.float32)
            inner = (s * intra_ref[h]).astype(jnp.bfloat16)
        qd = (q.astype(jnp.float32) * qdec_ref[h]).astype(jnp.bfloat16)
        y = jnp.dot(qd, state.astype(jnp.bfloat16), preferred_element_type=jnp.float32)
        if not reverse:
            y = jnp.dot(inner, v, preferred_element_type=jnp.float32) + y
        kd = (k.astype(jnp.float32) * kdec_ref[h]).astype(jnp.bfloat16)
        state_ref[h] = state * cdec_ref[h] + lax.dot_general(
            kd, v, (((0,), (0,)), ((), ())), preferred_element_type=jnp.float32)
        if reverse:
            y = y + yf_ref[:, h * dv:(h + 1) * dv].astype(jnp.float32)
            y = y * lax.rsqrt(jnp.mean(y * y, axis=-1, keepdims=True) + RMS_EPS)
            g = g_ref[:, h * dv:(h + 1) * dv].astype(jnp.float32)
            y = g * jax.nn.sigmoid(g) * y
        o_ref[:, h * dv:(h + 1) * dv] = y.astype(o_ref.dtype)


def _decay_tables(log_gamma, reverse):
    c = RET_CHUNK
    lg = log_gamma.astype(jnp.float32)[:, None, None]
    pos = jnp.arange(c, dtype=jnp.float32)
    rel = pos[:, None] - pos[None, :]
    scale = RET_QK_DIM ** -0.5
    if reverse:
        intra = jnp.where(rel < 0, jnp.exp(lg * jnp.maximum(-rel, 0.0)[None]), 0.0)
        qdec = jnp.exp(lg * (c - pos)[None, :, None])
        kdec = jnp.exp(lg * pos[None, :, None])
    else:
        intra = jnp.where(rel >= 0, jnp.exp(lg * jnp.maximum(rel, 0.0)[None]), 0.0)
        qdec = jnp.exp(lg * (pos + 1.0)[None, :, None])
        kdec = jnp.exp(lg * (c - 1.0 - pos)[None, :, None])
    return intra * scale, qdec, kdec * scale, jnp.exp(lg * c)


def retention(proj, seq_lens, decay, reverse, y_fwd=None, decay_reverse=None):
    t = proj.shape[0]
    c = RET_CHUNK
    nc = t // c
    bounds = np.cumsum([0] + [s // c for s in seq_lens])
    reset = np.zeros((nc,), np.int32)
    if reverse:
        reset[nc - bounds[1:]] = 1
        chunk = lambda i, r: nc - 1 - i
    else:
        reset[bounds[:-1]] = 1
        chunk = lambda i, r: i
    log_gamma = lambda dec: -jnp.exp(dec.astype(jnp.float32))
    intra, qdec, kdec, cdec = _decay_tables(log_gamma(decay), reverse)
    d, vw, h = D_MODEL, RET_V_WIDTH, RET_HEADS
    const = lambda shape: pl.BlockSpec(shape, lambda i, r: (0,) * len(shape))
    in_specs = [pl.BlockSpec((c, d), lambda i, r: (chunk(i, r), 0)),
                pl.BlockSpec((c, d), lambda i, r: (chunk(i, r), 1)),
                pl.BlockSpec((c, vw), lambda i, r: (chunk(i, r), 1)),
                const((h, c, 1)), const((h, c, 1)), const((h, 1, 1))]
    args = [jnp.asarray(reset), proj, proj, proj, qdec, kdec, cdec]
    if reverse:
        in_specs += [pl.BlockSpec((c, vw), lambda i, r: (chunk(i, r), 0)),
                     pl.BlockSpec((c, vw), lambda i, r: (chunk(i, r), 2))]
        args += [y_fwd, proj]
    else:
        in_specs += [const((h, c, c))]
        args += [intra + _decay_tables(log_gamma(decay_reverse), True)[0]]
    grid_spec = pltpu.PrefetchScalarGridSpec(
        num_scalar_prefetch=1, grid=(nc,), in_specs=in_specs,
        out_specs=pl.BlockSpec((c, vw), lambda i, r: (chunk(i, r), 0)),
        scratch_shapes=[pltpu.VMEM((h, RET_QK_DIM, RET_V_DIM), jnp.float32)])
    return pl.pallas_call(
        functools.partial(_retention_kernel, reverse=reverse),
        grid_spec=grid_spec,
        out_shape=jax.ShapeDtypeStruct((t, vw), jnp.bfloat16),
        compiler_params=_cparams("arbitrary"),
        name="retention_bwd" if reverse else "retention_fwd",
    )(*args)


ATT_BLOCK = 1024
ATT_HALF = 64
ATT_Q = 128
ATT_DEINT = 4
assert tuple(d for _, d in DIL_PATTERNS) == (1, ATT_DEINT, ATT_DEINT * ATT_DEINT)
assert all(w == 2 * ATT_HALF * d for w, d in DIL_PATTERNS)


def _attention_kernel(vprev_ref, vnext_ref, q_ref, kp_ref, kc_ref, kn_ref, vp_ref, vc_ref, vn_ref,
                      bias_ref, o_ref, q4_ref, k4_ref, v4_ref, kedge_ref, vedge_ref, edge_ref,
                      num1_ref, m1_ref, l1_ref, num4_ref, m4_ref, l4_ref, onat_ref,
                      q16_ref, k16_ref, v16_ref, num16_ref, m16_ref, l16_ref):
    i = pl.program_id(0)
    b = ATT_BLOCK
    g = ATT_DEINT
    bq = b // g
    h = ATT_HALF
    half_lane = ATT_HEAD_DIM
    no_prev = vprev_ref[i] == 0
    no_next = vnext_ref[i] == 0
    scale = ATT_HEAD_DIM ** -0.5 * math.log2(math.e)
    kv_blocks = ((kp_ref, vp_ref), (kc_ref, vc_ref), (kn_ref, vn_ref))

    for c in range(g):
        q4_ref[c] = q_ref[pl.ds(c, bq, stride=g), :]
        for blk, (k_blk, v_blk) in enumerate(kv_blocks):
            k4_ref[c, blk * bq:(blk + 1) * bq] = k_blk[pl.ds(c, bq, stride=g), :]
            v4_ref[c, blk * bq:(blk + 1) * bq] = v_blk[pl.ds(c, bq, stride=g), :]

    nk1 = ATT_Q + 2 * h
    for edge_buf, (prv, cur, nxt) in ((kedge_ref, (kp_ref, kc_ref, kn_ref)),
                                      (vedge_ref, (vp_ref, vc_ref, vn_ref))):
        edge_buf[0, 0:h] = prv[b - h:b]
        edge_buf[0, h:nk1] = cur[0:nk1 - h]
        edge_buf[1, 0:nk1 - h] = cur[b - (nk1 - h):b]
        edge_buf[1, nk1 - h:nk1] = nxt[0:h]

    for br, (_, d) in enumerate(DIL_PATTERNS):
        nq = min(ATT_Q, b // d)
        nk = nq + 2 * h
        col = lax.broadcasted_iota(jnp.int32, (1, nk), 1)
        before = jnp.where((col < h) & no_prev, NEG_INF, 0.0)
        after = jnp.where((col >= nk - h) & no_next, NEG_INF, 0.0)
        if b // d == nq:
            before = before + after
        for hd in range(2):
            edge_ref[br, 0, hd, 0:nq, 0:nk] = bias_ref[br, hd, 0:nq, 0:nk] + before
            edge_ref[br, 1, hd, 0:nq, 0:nk] = bias_ref[br, hd, 0:nq, 0:nk] + after

    def bias_of(br, j, nsub, nq, nk):
        if j == 0:
            return [edge_ref[br, 0, hd, 0:nq, 0:nk] for hd in range(2)]
        if j == nsub - 1:
            return [edge_ref[br, 1, hd, 0:nq, 0:nk] for hd in range(2)]
        return [bias_ref[br, hd, 0:nq, 0:nk] for hd in range(2)]

    def scores(load_q, load_k, bias):
        q = load_q()
        nq = q.shape[0]
        first = lax.broadcasted_iota(jnp.int32, (nq, LANES), 1) < half_lane
        q = q * scale
        q2 = jnp.concatenate([jnp.where(first, q, 0.0), jnp.where(first, 0.0, q)], axis=0)
        s = lax.dot_general(q2.astype(jnp.bfloat16), load_k().astype(jnp.bfloat16),
                            (((1,), (1,)), ((), ())), preferred_element_type=jnp.float32)
        return s + jnp.concatenate(bias(), axis=0)

    def softmax(s):
        m = jnp.max(s, axis=-1, keepdims=True)
        p = jnp.exp2(s - m)
        return p.astype(jnp.bfloat16), m, jnp.sum(p, axis=-1, keepdims=True)

    def values(p, m, l, load_v, store):
        nq = p.shape[0] // 2
        first = lax.broadcasted_iota(jnp.int32, (nq, LANES), 1) < half_lane
        pv = jnp.dot(p, load_v().astype(jnp.bfloat16), preferred_element_type=jnp.float32)
        store(jnp.where(first, pv[:nq], pv[nq:]), jnp.where(first, m[:nq], m[nq:]),
              jnp.where(first, l[:nq], l[nq:]))

    units = []

    def store_to(num_r, m_r, l_r, idx):
        def store(num, m, l):
            num_r[idx] = num
            m_r[idx] = m
            l_r[idx] = l
        return store

    nq, nk, nsub = ATT_Q, nk1, b // ATT_Q
    for j in range(nsub):
        rows = slice(j * nq, (j + 1) * nq)
        keys = slice(j * nq - h, j * nq - h + nk)
        if j == 0:
            load_k, load_v = (lambda: kedge_ref[0]), (lambda: vedge_ref[0])
        elif j == nsub - 1:
            load_k, load_v = (lambda: kedge_ref[1]), (lambda: vedge_ref[1])
        else:
            load_k, load_v = (lambda keys=keys: kc_ref[keys]), (lambda keys=keys: vc_ref[keys])
        units.append(((lambda rows=rows: q_ref[rows]), load_k, load_v,
                      (lambda j=j, a=(nsub, nq, nk): bias_of(0, j, *a)),
                      store_to(num1_ref, m1_ref, l1_ref, rows)))

    nq = min(ATT_Q, bq)
    nk, nsub = nq + 2 * h, bq // nq
    for c in range(g):
        for j in range(nsub):
            rows = slice(j * nq, (j + 1) * nq)
            keys = slice(bq + j * nq - h, bq + j * nq - h + nk)
            units.append(((lambda c=c, rows=rows: q4_ref[c, rows]),
                          (lambda c=c, keys=keys: k4_ref[c, keys]),
                          (lambda c=c, keys=keys: v4_ref[c, keys]),
                          (lambda j=j, a=(nsub, nq, nk): bias_of(1, j, *a)),
                          store_to(num4_ref, m4_ref, l4_ref, (0, c, rows))))

    nq = bq // g
    nk = nq + 2 * h
    for c in range(g):
        for a in range(g):
            q16_ref[c, a] = q4_ref[c, pl.ds(a, nq, stride=g)]
            k16_ref[c, a] = k4_ref[c, pl.ds(a, nk, stride=g)]
            v16_ref[c, a] = v4_ref[c, pl.ds(a, nk, stride=g)]
            units.append(((lambda c=c, a=a: q16_ref[c, a]), (lambda c=c, a=a: k16_ref[c, a]),
                          (lambda c=c, a=a: v16_ref[c, a]),
                          (lambda a_=(1, nq, nk): bias_of(2, 0, *a_)),
                          store_to(num16_ref, m16_ref, l16_ref, (c, a))))

    s_prev = None
    sm_prev = None
    for t in range(len(units) + 2):
        s_new = scores(units[t][0], units[t][1], units[t][3]) if t < len(units) else None
        sm_new = softmax(s_prev) if s_prev is not None else None
        if sm_prev is not None:
            values(*sm_prev, units[t - 2][2], units[t - 2][4])
        s_prev, sm_prev = s_new, sm_new

    for c in range(g):
        for a in range(g):
            rows = pl.ds(a, nq, stride=g)
            num4_ref[1, c, rows] = num16_ref[c, a]
            m4_ref[1, c, rows] = m16_ref[c, a]
            l4_ref[1, c, rows] = l16_ref[c, a]

    for c in range(g):
        rows = pl.ds(c, bq, stride=g)
        ms = [m1_ref[rows], m4_ref[0, c], m4_ref[1, c]]
        ls = [l1_ref[rows], l4_ref[0, c], l4_ref[1, c]]
        nums = [num1_ref[rows], num4_ref[0, c], num4_ref[1, c]]
        m_max = jnp.maximum(jnp.maximum(ms[0], ms[1]), ms[2])
        wts = [jnp.exp2(mm - m_max) for mm in ms]
        den = wts[0] * ls[0] + wts[1] * ls[1] + wts[2] * ls[2]
        num = wts[0] * nums[0] + wts[1] * nums[1] + wts[2] * nums[2]
        onat_ref[rows] = num / den
    o_ref[...] = onat_ref[...].astype(o_ref.dtype)


def _alibi_bias():
    slopes = jnp.exp2(-8.0 * jnp.arange(1, ATT_HEADS + 1, dtype=jnp.float32) / ATT_HEADS)
    qi = jnp.arange(ATT_Q)
    ki = jnp.arange(ATT_Q + 2 * ATT_HALF) - ATT_HALF
    rel = jnp.abs(ki[None, :] - qi[:, None])
    dil = jnp.asarray([d for _, d in DIL_PATTERNS], jnp.float32)
    bias = -slopes[:, None, None, None] * (dil[None, :, None, None] * rel.astype(jnp.float32)[None, None])
    bias = jnp.where((rel <= ATT_HALF)[None, None], bias * math.log2(math.e), NEG_INF)
    return bias.reshape(ATT_HEADS // 2, 2, len(DIL_PATTERNS), *rel.shape).transpose(0, 2, 1, 3, 4)


def dilated_attention(qkv, seq_lens):
    t = qkv.shape[0]
    b = ATT_BLOCK
    g = ATT_DEINT
    nb = t // b
    bounds = np.cumsum([0] + [s // b for s in seq_lens])
    vprev = np.ones((nb,), np.int32)
    vnext = np.ones((nb,), np.int32)
    vprev[bounds[:-1]] = 0
    vnext[bounds[1:] - 1] = 0
    pairs = ATT_HEADS // 2
    blk = lambda which, off: pl.BlockSpec(
        (b, LANES), lambda i, hp, vp, vn: (jnp.clip(i + which, 0, nb - 1), off + hp))
    bias = _alibi_bias()
    grid_spec = pltpu.PrefetchScalarGridSpec(
        num_scalar_prefetch=2, grid=(nb, pairs),
        in_specs=[blk(0, 0),
                  blk(-1, pairs), blk(0, pairs), blk(1, pairs),
                  blk(-1, 2 * pairs), blk(0, 2 * pairs), blk(1, 2 * pairs),
                  pl.BlockSpec((None,) + bias.shape[1:], lambda i, hp, vp, vn: (hp, 0, 0, 0, 0))],
        out_specs=pl.BlockSpec((b, LANES), lambda i, hp, vp, vn: (i, hp)),
        scratch_shapes=[pltpu.VMEM((g, b // g, LANES), jnp.float32),
                        pltpu.VMEM((g, 3 * b // g, LANES), jnp.float32),
                        pltpu.VMEM((g, 3 * b // g, LANES), jnp.float32),
                        pltpu.VMEM((2, ATT_Q + 2 * ATT_HALF, LANES), jnp.float32),
                        pltpu.VMEM((2, ATT_Q + 2 * ATT_HALF, LANES), jnp.float32),
                        pltpu.VMEM((len(DIL_PATTERNS), 2) + bias.shape[2:], jnp.float32)]
        + [pltpu.VMEM((b, LANES), jnp.float32)] * 3
        + [pltpu.VMEM((2, g, b // g, LANES), jnp.float32)] * 3
        + [pltpu.VMEM((b, LANES), jnp.float32)]
        + [pltpu.VMEM((g, g, b // (g * g), LANES), jnp.float32),
           pltpu.VMEM((g, g, b // (g * g) + 2 * ATT_HALF, LANES), jnp.float32),
           pltpu.VMEM((g, g, b // (g * g) + 2 * ATT_HALF, LANES), jnp.float32)]
        + [pltpu.VMEM((g, g, b // (g * g), LANES), jnp.float32)] * 3)
    return pl.pallas_call(
        _attention_kernel,
        grid_spec=grid_spec,
        out_shape=jax.ShapeDtypeStruct((t, D_MODEL), jnp.bfloat16),
        compiler_params=_cparams("parallel", "arbitrary"),
        name="dilated_attention",
    )(jnp.asarray(vprev), jnp.asarray(vnext), qkv, qkv, qkv, qkv, qkv, qkv, qkv, bias)


PROJ_COLS = 1024


def kernel(x_prompt, x_sample, norm_mix, norm_ffn, norm_final, ret_w_in, ret_w_out, ret_decay_fwd,
           ret_decay_bwd, att_w_qkv, att_w_out, moe_router, moe_w_gate, moe_w_up, moe_w_down):
    d = x_prompt.shape[-1]
    depth = norm_mix.shape[0]
    tp = x_prompt.shape[0] * x_prompt.shape[1]
    ts = x_sample.shape[0] * x_sample.shape[1]
    seq_lens = [x_prompt.shape[1]] * x_prompt.shape[0] + [x_sample.shape[1]] * x_sample.shape[0]
    groups = ((0, tp), (tp, ts))
    bf16 = lambda w: w.astype(jnp.bfloat16)

    x = (x_prompt.reshape(tp, d), x_sample.reshape(ts, d))
    for i in range(depth):
        j = i // 2
        last = i == depth - 1
        if i % 2 == 0:
            proj = norm_matmul(x, norm_mix[i], bf16(ret_w_in[j]), jnp.bfloat16, PROJ_COLS)
            y_fwd = retention(proj, seq_lens, ret_decay_fwd[j], False, decay_reverse=ret_decay_bwd[j])
            z = retention(proj, seq_lens, ret_decay_bwd[j], True, y_fwd)
            w_out = ret_w_out[j]
        else:
            qkv = norm_matmul(x, norm_mix[i], bf16(att_w_qkv[j]), jnp.float32, PROJ_COLS)
            z = dilated_attention(qkv, seq_lens)
            w_out = att_w_out[j]
        x1, hn, aff = outproj_router(z, bf16(w_out), x, norm_ffn[i], moe_router[i])
        x = moe_layer(x1, hn, aff, groups, moe_w_gate, moe_w_up, moe_w_down, i,
                      norm_final, final=last, out_rows=(tp, ts) if last else (tp + ts,))
    return x[0].reshape(x_prompt.shape), x[1].reshape(x_sample.shape)
```

```python
import functools
import math

import jax
import jax.numpy as jnp
import numpy as np
from jax import lax
from jax.experimental import pallas as pl
from jax.experimental.pallas import tpu as pltpu

D_MODEL = 1024
RET_HEADS = 4
RET_QK_DIM = D_MODEL // RET_HEADS
RET_V_WIDTH = 2 * D_MODEL
RET_V_DIM = RET_V_WIDTH // RET_HEADS
ATT_HEADS = 16
ATT_HEAD_DIM = D_MODEL // ATT_HEADS
DIL_PATTERNS = ((128, 1), (512, 4), (2048, 16))
N_EXPERTS = 16
EC_CAPACITY_FACTOR = 2
RMS_EPS = 1e-6
NEG_INF = -1e30

LANES = 128
BF16_ROWS = 16
VMEM_LIMIT = 56 * 1024 * 1024

ROW_TILE = 512
MOE_TILE = 256
MOE_WIN = 64
MOE_MAX_PASSES = MOE_TILE // MOE_WIN + 1
FFN_ROWS = 512
FFN_CHUNK = 256


def _cparams(*sem):
    return pltpu.CompilerParams(dimension_semantics=sem, vmem_limit_bytes=VMEM_LIMIT)


def _part_layout(parts, tile):
    specs, firsts, first = [], [], 0
    for p in parts:
        n = p.shape[0] // tile
        specs.append(pl.BlockSpec((tile, p.shape[1]),
                                  lambda i, *_, first=first, n=n: (jnp.clip(i - first, 0, n - 1), 0)))
        firsts.append(first)
        first += n
    return specs, tuple(firsts), first


def _read_parts(i, refs, firsts):
    x = refs[0][...]
    for ref, first in zip(refs[1:], firsts[1:]):
        x = jnp.where(i >= first, ref[...], x)
    return x


def _norm_matmul_kernel(*refs, tn, firsts):
    x_refs, (g_ref, w_ref, o_ref) = refs[:len(firsts)], refs[len(firsts):]
    x = _read_parts(pl.program_id(0), x_refs, firsts)
    ms = jnp.mean(x * x, axis=-1, keepdims=True)
    hn = (x * lax.rsqrt(ms + RMS_EPS) * g_ref[...]).astype(jnp.bfloat16)
    for c in range(w_ref.shape[1] // tn):
        cols = slice(c * tn, (c + 1) * tn)
        o_ref[:, cols] = jnp.dot(hn, w_ref[:, cols],
                                 preferred_element_type=jnp.float32).astype(o_ref.dtype)


def norm_matmul(x_parts, g, w_bf16, out_dtype, tn):
    d, n = w_bf16.shape
    x_specs, firsts, tiles = _part_layout(x_parts, ROW_TILE)
    return pl.pallas_call(
        functools.partial(_norm_matmul_kernel, tn=tn, firsts=firsts),
        grid=(tiles,),
        in_specs=x_specs + [pl.BlockSpec((1, d), lambda i: (0, 0)),
                            pl.BlockSpec((d, n), lambda i: (0, 0), pipeline_mode=pl.Buffered(1))],
        out_specs=pl.BlockSpec((ROW_TILE, n), lambda i: (i, 0)),
        out_shape=jax.ShapeDtypeStruct((tiles * ROW_TILE, n), out_dtype),
        compiler_params=_cparams("parallel"),
        name="norm_matmul",
    )(*x_parts, g.reshape(1, d), w_bf16)


def _outproj_router_kernel(z_ref, w_ref, g_ref, wr_ref, *refs, firsts):
    x_refs, (x1_ref, hn_ref, aff_ref) = refs[:len(firsts)], refs[len(firsts):]
    x = _read_parts(pl.program_id(0), x_refs, firsts)
    x1 = x + jnp.dot(z_ref[...], w_ref[...], preferred_element_type=jnp.float32)
    x1_ref[...] = x1
    ms = jnp.mean(x1 * x1, axis=-1, keepdims=True)
    hn = x1 * lax.rsqrt(ms + RMS_EPS) * g_ref[...]
    hn_hi = hn.astype(jnp.bfloat16)
    hn_ref[...] = hn_hi
    hn_lo = (hn - hn_hi.astype(jnp.float32)).astype(jnp.bfloat16)
    wr = wr_ref[...]
    wr_hi = wr.astype(jnp.bfloat16)
    wr_lo = (wr - wr_hi.astype(jnp.float32)).astype(jnp.bfloat16)
    nt = (((1,), (1,)), ((), ()))
    e = wr.shape[0]
    by_hi = lax.dot_general(jnp.concatenate([wr_hi, wr_lo], axis=0), hn_hi, nt,
                            preferred_element_type=jnp.float32)
    logits = (by_hi[:e] + by_hi[e:]
              + lax.dot_general(wr_hi, hn_lo, nt, preferred_element_type=jnp.float32))
    m = jnp.max(logits, axis=0, keepdims=True)
    p = jnp.exp(logits - m)
    aff_ref[...] = p / jnp.sum(p, axis=0, keepdims=True)


def outproj_router(z, w_bf16, x_parts, g, w_router):
    t, k = z.shape
    d, e = w_router.shape
    x_specs, firsts, tiles = _part_layout(x_parts, ROW_TILE)
    assert tiles * ROW_TILE == t
    return pl.pallas_call(
        functools.partial(_outproj_router_kernel, firsts=firsts),
        grid=(tiles,),
        in_specs=[pl.BlockSpec((ROW_TILE, k), lambda i: (i, 0)),
                  pl.BlockSpec((k, d), lambda i: (0, 0)),
                  pl.BlockSpec((1, d), lambda i: (0, 0)),
                  pl.BlockSpec((e, d), lambda i: (0, 0))] + x_specs,
        out_specs=[pl.BlockSpec((ROW_TILE, d), lambda i: (i, 0)),
                   pl.BlockSpec((ROW_TILE, d), lambda i: (i, 0)),
                   pl.BlockSpec((e, ROW_TILE), lambda i: (0, i))],
        out_shape=[jax.ShapeDtypeStruct((t, d), jnp.float32),
                   jax.ShapeDtypeStruct((t, d), jnp.bfloat16),
                   jax.ShapeDtypeStruct((e, t), jnp.float32)],
        compiler_params=_cparams("parallel"),
        name="outproj_router",
    )(z, w_bf16, g.reshape(1, d), w_router.T, *x_parts)


def _select_kernel(aff_ref, slot_ref, rowstart_ref, *, cap, base):
    e, r, _ = aff_ref.shape
    bits = pltpu.bitcast(aff_ref[...], jnp.int32)

    def count(mask):
        c = jnp.sum(jnp.where(mask, 1.0, 0.0), axis=2, keepdims=True)
        return jnp.sum(c, axis=1, keepdims=True)

    def search(i, thr):
        cand = thr | jnp.left_shift(jnp.int32(1), 30 - i)
        return jnp.where(count(bits >= cand) >= cap, cand, thr)

    thr = lax.fori_loop(0, 31, search, jnp.zeros((e, 1, 1), jnp.int32))
    gt = bits > thr
    eq = bits == thr
    need = cap - count(gt)

    row_i = lax.broadcasted_iota(jnp.int32, (LANES, LANES), 0)
    col_i = lax.broadcasted_iota(jnp.int32, (LANES, LANES), 1)
    upper = jnp.where(row_i <= col_i, 1.0, 0.0).astype(jnp.bfloat16)
    ones = jnp.ones((LANES, LANES), jnp.bfloat16)
    rr = lax.broadcasted_iota(jnp.int32, (r, r), 0)
    rc = lax.broadcasted_iota(jnp.int32, (r, r), 1)
    lower = jnp.where(rc < rr, 1.0, 0.0).astype(jnp.bfloat16)

    def excl_cumsum(mask):
        m = jnp.where(mask, 1.0, 0.0).astype(jnp.bfloat16).reshape(e * r, LANES)
        incl = jnp.dot(m, upper, preferred_element_type=jnp.float32)
        tot = jnp.dot(m, ones, preferred_element_type=jnp.float32)
        offs = []
        for ee in range(e):
            t_e = tot[ee * r:(ee + 1) * r].astype(jnp.bfloat16)
            offs.append(jnp.dot(lower, t_e, preferred_element_type=jnp.float32))
        off = jnp.concatenate(offs, axis=0)
        excl = incl - m.astype(jnp.float32) + off
        return excl.reshape(e, r, LANES), off.reshape(e, r, LANES)

    eq_rank, _ = excl_cumsum(eq)
    sel = gt | (eq & (eq_rank < need))
    pos, off = excl_cumsum(sel)
    slot_ref[...] = jnp.where(sel, pos.astype(jnp.int32) + base, -1)
    rowstart_ref[...] = off.astype(jnp.int32) + base


def select_tokens(aff3, cap, base):
    e, r, _ = aff3.shape
    return pl.pallas_call(
        functools.partial(_select_kernel, cap=cap, base=base),
        out_shape=[jax.ShapeDtypeStruct((e, r, LANES), jnp.int32),
                   jax.ShapeDtypeStruct((e, r, LANES), jnp.int32)],
        compiler_params=pltpu.CompilerParams(vmem_limit_bytes=VMEM_LIMIT),
        name="select_tokens",
    )(aff3)


def _dispatch_kernel(starts_ref, slot_ref, hn_ref, gp_ref, xg_ref, win_ref, carry_ref, sem_ref):
    i = pl.program_id(0)
    nt = pl.num_programs(0)
    e = slot_ref.shape[0]
    tt = slot_ref.shape[1]
    w = MOE_WIN
    buf = i % 2

    def aligned(s):
        return (s // BF16_ROWS) * BF16_ROWS

    a = [aligned(starts_ref[i * e + ee]) for ee in range(e)]
    end = [starts_ref[(i + 1) * e + ee] for ee in range(e)]
    n_pass = jnp.int32(1)
    for ee in range(e):
        n_pass = jnp.maximum(n_pass, (end[ee] - a[ee] + (w - 1)) // w)

    @pl.when(i == 0)
    def _():
        carry_ref[...] = jnp.zeros_like(carry_ref)
        win_ref[1, 0] = jnp.zeros(win_ref.shape[2:], win_ref.dtype)
        tail = [pltpu.make_async_copy(win_ref.at[1, 0], xg_ref.at[ee, pl.ds(r0, w), :], sem_ref.at[1, ee])
                for ee in range(e) for r0 in range(xg_ref.shape[1] - MOE_MAX_PASSES * w, xg_ref.shape[1], w)]
        for c in tail:
            c.start()
        for c in tail:
            c.wait()

    def copies(b, p):
        return [pltpu.make_async_copy(
            win_ref.at[b, ee],
            xg_ref.at[ee, pl.ds(pl.multiple_of(a[ee] + p * w, BF16_ROWS), w), :],
            sem_ref.at[b, ee]) for ee in range(e)]

    def wait_tile(b, src_i):
        for ee in range(e):
            pltpu.make_async_copy(win_ref.at[b, ee], xg_ref.at[ee, pl.ds(0, w), :],
                                  sem_ref.at[b, ee]).wait()

    hn = hn_ref[...]
    row = lax.broadcasted_iota(jnp.int32, (w, tt), 0)

    def one_pass(p, _):
        onehot = []
        for ee in range(e):
            rel = slot_ref[pl.ds(ee, 1), :] - (a[ee] + p * w)
            onehot.append(jnp.where(row == rel, 1.0, 0.0).astype(jnp.bfloat16))
        onehot = jnp.concatenate(onehot, axis=0)
        rows = jnp.concatenate([jnp.dot(onehot, hn, preferred_element_type=jnp.float32),
                                jnp.dot(onehot, gp_ref[...], preferred_element_type=jnp.float32)], axis=1)

        for ee in range(e):
            r_e = rows[ee * w:(ee + 1) * w]
            head = r_e[:BF16_ROWS] + jnp.where(p == 0, carry_ref[ee].astype(jnp.float32), 0.0)
            win_ref[buf, ee, pl.ds(0, BF16_ROWS), :] = head.astype(jnp.bfloat16)
            win_ref[buf, ee, pl.ds(BF16_ROWS, w - BF16_ROWS), :] = r_e[BF16_ROWS:].astype(jnp.bfloat16)

        for ee in range(e):
            nxt = aligned(end[ee]) - (a[ee] + p * w)
            held = win_ref[buf, ee, pl.ds(pl.multiple_of(jnp.clip(nxt, 0, w - BF16_ROWS), BF16_ROWS),
                                          BF16_ROWS), :]
            keep = (nxt >= 0) & (nxt < w)
            clear = (p == n_pass - 1) & (nxt >= w)
            carry_ref[ee] = jnp.where(keep, held, jnp.where(clear, jnp.zeros_like(held), carry_ref[ee]))

        @pl.when((p == 0) & (i > 0))
        def _():
            wait_tile(1 - buf, i - 1)

        for c in copies(buf, p):
            c.start()

        @pl.when(p + 1 < n_pass)
        def _():
            for c in copies(buf, p):
                c.wait()
        return 0

    lax.fori_loop(0, n_pass, one_pass, 0)

    @pl.when(i == nt - 1)
    def _():
        wait_tile(buf, i)


def dispatch(starts, slot, hn, gate_pieces, rows_padded):
    e, t = slot.shape
    d = hn.shape[1] + gate_pieces.shape[1]
    grid_spec = pltpu.PrefetchScalarGridSpec(
        num_scalar_prefetch=1,
        grid=(t // MOE_TILE,),
        in_specs=[pl.BlockSpec((e, MOE_TILE), lambda i, s: (0, i)),
                  pl.BlockSpec((MOE_TILE, hn.shape[1]), lambda i, s: (i, 0)),
                  pl.BlockSpec((MOE_TILE, gate_pieces.shape[1]), lambda i, s: (i, 0))],
        out_specs=pl.BlockSpec(memory_space=pl.ANY),
        scratch_shapes=[pltpu.VMEM((2, e, MOE_WIN, d), jnp.bfloat16),
                        pltpu.VMEM((e, BF16_ROWS, d), jnp.bfloat16),
                        pltpu.SemaphoreType.DMA((2, e))])
    return pl.pallas_call(
        _dispatch_kernel,
        grid_spec=grid_spec,
        out_shape=jax.ShapeDtypeStruct((e, rows_padded, d), jnp.bfloat16),
        compiler_params=_cparams("arbitrary"),
        name="moe_dispatch",
    )(starts, slot, hn, gate_pieces)


FFN_SLABS = 16


def _ffn_kernel(x_ref, wg_hbm, wu_hbm, wd_hbm, y_ref, wg_ref, wu_ref, wd_ref, sg_ref, su_ref, sd_ref,
                acc_ref, sem_ref, *, layer, steps):
    ee = pl.program_id(0)
    m = pl.program_id(1)
    d, f = acc_ref.shape[1], wg_ref.shape[2]
    in_rows, hid_rows = sg_ref.shape[0], sd_ref.shape[0]
    cur = ee % 2

    def slab_copies(expert, c):
        r_in = pl.ds(pl.multiple_of(c * in_rows, BF16_ROWS), in_rows)
        r_hid = pl.ds(pl.multiple_of(c * hid_rows, BF16_ROWS), hid_rows)
        return [pltpu.make_async_copy(wg_hbm.at[layer, expert, r_in, :], sg_ref, sem_ref.at[0]),
                pltpu.make_async_copy(wu_hbm.at[layer, expert, r_in, :], su_ref, sem_ref.at[1]),
                pltpu.make_async_copy(wd_hbm.at[layer, expert, r_hid, :], sd_ref, sem_ref.at[2])]

    def land(copies, half, c):
        for cp in copies:
            cp.wait()
        r_in = pl.ds(pl.multiple_of(c * in_rows, BF16_ROWS), in_rows)
        r_hid = pl.ds(pl.multiple_of(c * hid_rows, BF16_ROWS), hid_rows)
        wg_ref[half, r_in, :] = sg_ref[...].astype(jnp.bfloat16)
        wu_ref[half, r_in, :] = su_ref[...].astype(jnp.bfloat16)
        wd_ref[half, r_hid, :] = sd_ref[...].astype(jnp.bfloat16)

    @pl.when((ee == 0) & (m == 0))
    def _():
        def fetch(c, _):
            copies = slab_copies(0, c)
            for cp in copies:
                cp.start()
            land(copies, 0, c)
            return 0
        lax.fori_loop(0, FFN_SLABS, fetch, 0)

    per_step = -(-FFN_SLABS // steps)
    first = m * per_step
    has_next = ee + 1 < pl.num_programs(0)
    prefetch = has_next & (first < FFN_SLABS)
    src_expert = jnp.minimum(ee + 1, pl.num_programs(0) - 1)
    copies = slab_copies(src_expert, jnp.minimum(first, FFN_SLABS - 1))
    for cp in copies:
        cp.start()

    x = x_ref[:, 0:d]
    n_chunks = f // FFN_CHUNK
    for c in range(n_chunks):
        if c == n_chunks // 2:
            land(copies, 1 - cur, jnp.where(prefetch, first, FFN_SLABS))
        cols = slice(c * FFN_CHUNK, (c + 1) * FFN_CHUNK)
        g = jnp.dot(x, wg_ref[cur, 0:d, cols], preferred_element_type=jnp.float32)
        u = jnp.dot(x, wu_ref[cur, 0:d, cols], preferred_element_type=jnp.float32)
        h = (g * jax.nn.sigmoid(g) * u).astype(jnp.bfloat16)
        part = jnp.dot(h, wd_ref[cur, cols, :], preferred_element_type=jnp.float32)
        if c == 0:
            acc_ref[...] = part
        else:
            acc_ref[...] += part
    pieces = x_ref[:, d:].astype(jnp.float32)
    lane = lax.broadcasted_iota(jnp.int32, pieces.shape, 1)
    n_exp = pl.num_programs(0)
    mine = (lane == ee) | (lane == n_exp + ee) | (lane == 2 * n_exp + ee)
    gate = jnp.sum(jnp.where(mine, pieces, 0.0), axis=1, keepdims=True)
    y_ref[...] = (acc_ref[...] * gate).astype(y_ref.dtype)

    for k in range(1, per_step):
        @pl.when(has_next & (first + k < FFN_SLABS))
        def _():
            more = slab_copies(ee + 1, first + k)
            for cp in more:
                cp.start()
            land(more, 1 - cur, first + k)


def expert_ffn(xg, wg, wu, wd, layer, rows):
    e, _, dx = xg.shape
    d, f = wg.shape[2], wg.shape[3]
    in_rows, hid_rows = d // FFN_SLABS, f // FFN_SLABS
    assert in_rows % BF16_ROWS == 0 and hid_rows % BF16_ROWS == 0 and f % FFN_CHUNK == 0
    return pl.pallas_call(
        functools.partial(_ffn_kernel, layer=layer, steps=rows // FFN_ROWS),
        grid=(e, rows // FFN_ROWS),
        in_specs=[pl.BlockSpec((None, FFN_ROWS, dx), lambda ee, m: (ee, m, 0)),
                  pl.BlockSpec(memory_space=pl.ANY),
                  pl.BlockSpec(memory_space=pl.ANY),
                  pl.BlockSpec(memory_space=pl.ANY)],
        out_specs=pl.BlockSpec((None, FFN_ROWS, d), lambda ee, m: (ee, m, 0)),
        out_shape=jax.ShapeDtypeStruct((e, rows, d), jnp.bfloat16),
        scratch_shapes=[pltpu.VMEM((2, d + in_rows, f), jnp.bfloat16),
                        pltpu.VMEM((2, d + in_rows, f), jnp.bfloat16),
                        pltpu.VMEM((2, f + hid_rows, d), jnp.bfloat16),
                        pltpu.VMEM((in_rows, f), jnp.float32),
                        pltpu.VMEM((in_rows, f), jnp.float32),
                        pltpu.VMEM((hid_rows, d), jnp.float32),
                        pltpu.VMEM((FFN_ROWS, d), jnp.float32),
                        pltpu.SemaphoreType.DMA((3,))],
        compiler_params=_cparams("arbitrary", "arbitrary"),
        name="expert_ffn",
    )(xg, wg, wu, wd)


def _combine_kernel(starts_ref, slot_ref, x_ref, gain_ref, ys_ref, *refs, rows, final, out_tiles):
    o_refs, (buf_ref, sem_ref) = refs[:len(out_tiles)], refs[len(out_tiles):]
    i = pl.program_id(0)
    nt = pl.num_programs(0)
    tt, e = slot_ref.shape
    w = MOE_WIN
    d = x_ref.shape[1]
    b = i % 2

    def aligned(s):
        return (s // BF16_ROWS) * BF16_ROWS

    def window_start(ti, ee, p):
        return jnp.minimum(aligned(starts_ref[ti * e + ee]) + p * w, rows - w)

    def copies(ti, bb, p):
        return [pltpu.make_async_copy(
            ys_ref.at[ee, pl.ds(pl.multiple_of(window_start(ti, ee, p), BF16_ROWS), w), :],
            buf_ref.at[bb, pl.ds(ee * w, w), :],
            sem_ref.at[bb, ee]) for ee in range(e)]

    @pl.when(i == 0)
    def _():
        for c in copies(i, b, 0):
            c.start()

    @pl.when(i + 1 < nt)
    def _():
        for c in copies(i + 1, 1 - b, 0):
            c.start()

    n_pass = jnp.int32(1)
    for ee in range(e):
        n_pass = jnp.maximum(
            n_pass, (starts_ref[(i + 1) * e + ee] - aligned(starts_ref[i * e + ee]) + (w - 1)) // w)

    slot = slot_ref[...]
    expert = lax.broadcasted_iota(jnp.int32, (1, e), 1)
    spread = jnp.where(lax.broadcasted_iota(jnp.int32, (e, e * w), 1) // w
                       == lax.broadcasted_iota(jnp.int32, (e, e * w), 0), 1.0, 0.0).astype(jnp.bfloat16)
    col_in_window = (lax.broadcasted_iota(jnp.int32, (tt, e * w), 1) % w).astype(jnp.float32)

    def one_pass(p, acc):
        @pl.when(p > 0)
        def _():
            for c in copies(i, b, p):
                c.start()

        for c in copies(i, b, p):
            c.wait()

        lo = jnp.zeros((1, e), jnp.int32)
        ws = jnp.zeros((1, e), jnp.int32)
        for ee in range(e):
            lo = jnp.where(expert == ee, aligned(starts_ref[i * e + ee]) + p * w, lo)
            ws = jnp.where(expert == ee, window_start(i, ee, p), ws)
        ok = (slot >= lo) & (slot < lo + w)
        rel = jnp.where(ok, slot - ws, -1).astype(jnp.float32).astype(jnp.bfloat16)
        rel_cols = jnp.dot(rel, spread, preferred_element_type=jnp.float32)
        onehot = jnp.where(rel_cols == col_in_window, 1.0, 0.0).astype(jnp.bfloat16)
        return acc + jnp.dot(onehot, buf_ref[b], preferred_element_type=jnp.float32)

    moe = lax.fori_loop(0, n_pass, one_pass, jnp.zeros((tt, d), jnp.float32))
    x = x_ref[...] + moe
    if final:
        ms = jnp.mean(x * x, axis=-1, keepdims=True)
        x = x * lax.rsqrt(ms + RMS_EPS) * gain_ref[...]
    first = 0
    for o_ref, n in zip(o_refs, out_tiles):
        @pl.when((i >= first) & (i < first + n))
        def _(o_ref=o_ref):
            o_ref[...] = x
        first += n


def combine(starts, slot_t, x, gain, ys, final, out_rows):
    t, e = slot_t.shape
    d = x.shape[1]
    rows = ys.shape[1]
    out_tiles = tuple(r // MOE_TILE for r in out_rows)
    assert sum(out_rows) == t
    out_specs, first = [], 0
    for n in out_tiles:
        out_specs.append(pl.BlockSpec((MOE_TILE, d),
                                      lambda i, s, first=first, n=n: (jnp.clip(i - first, 0, n - 1), 0)))
        first += n
    grid_spec = pltpu.PrefetchScalarGridSpec(
        num_scalar_prefetch=1,
        grid=(t // MOE_TILE,),
        in_specs=[pl.BlockSpec((MOE_TILE, e), lambda i, s: (i, 0)),
                  pl.BlockSpec((MOE_TILE, d), lambda i, s: (i, 0)),
                  pl.BlockSpec((1, d), lambda i, s: (0, 0)),
                  pl.BlockSpec(memory_space=pl.ANY)],
        out_specs=out_specs,
        scratch_shapes=[pltpu.VMEM((2, e * MOE_WIN, d), jnp.bfloat16),
                        pltpu.SemaphoreType.DMA((2, e))])
    return pl.pallas_call(
        functools.partial(_combine_kernel, rows=rows, final=final, out_tiles=out_tiles),
        grid_spec=grid_spec,
        out_shape=[jax.ShapeDtypeStruct((r, d), jnp.float32) for r in out_rows],
        compiler_params=_cparams("arbitrary"),
        name="moe_combine",
    )(starts, slot_t, x, gain.reshape(1, d), ys)


def moe_layer(x1, hn, aff, groups, wg, wu, wd, layer, gain, final, out_rows):
    e, t = aff.shape
    slots, rowstarts = [], []
    base = 0
    for (t0, tg) in groups:
        cap = EC_CAPACITY_FACTOR * tg // e
        aff3 = aff[:, t0:t0 + tg].reshape(e, tg // LANES, LANES)
        s, r = select_tokens(aff3, cap, base)
        slots.append(s.reshape(e, tg))
        rowstarts.append(r[:, ::MOE_TILE // LANES, 0])
        base += cap
    rows = base
    slot = jnp.concatenate(slots, axis=1)
    starts = jnp.concatenate(rowstarts + [jnp.full((e, 1), rows, jnp.int32)], axis=1)
    starts = starts.T.reshape(-1)
    gate = aff.T
    g_hi = gate.astype(jnp.bfloat16)
    g_mid = (gate - g_hi.astype(jnp.float32)).astype(jnp.bfloat16)
    g_lo = (gate - g_hi.astype(jnp.float32) - g_mid.astype(jnp.float32)).astype(jnp.bfloat16)
    pieces = jnp.concatenate([g_hi, g_mid, g_lo, jnp.zeros((t, LANES - 3 * e), jnp.bfloat16)], axis=1)
    xg = dispatch(starts, slot, hn, pieces, rows + MOE_MAX_PASSES * MOE_WIN)
    ys = expert_ffn(xg, wg, wu, wd, layer, rows)
    return tuple(combine(starts, slot.T, x1, gain, ys, final, out_rows))


RET_CHUNK = 256


def _retention_kernel(reset_ref, q_ref, k_ref, v_ref, qdec_ref, kdec_ref, cdec_ref, *rest, reverse):
    if reverse:
        yf_ref, g_ref, o_ref, state_ref = rest
    else:
        intra_ref, o_ref, state_ref = rest
    i = pl.program_id(0)

    @pl.when(reset_ref[i] == 1)
    def _():
        state_ref[...] = jnp.zeros_like(state_ref)

    dk, dv = RET_QK_DIM, RET_V_DIM
    for h in range(RET_HEADS):
        q = q_ref[:, h * dk:(h + 1) * dk]
        k = k_ref[:, h * dk:(h + 1) * dk]
        v = v_ref[:, h * dv:(h + 1) * dv]
        state = state_ref[h]
        if not reverse:
            s = lax.dot_general(q, k, (((1,), (1,)), ((), ())), preferred_element_type=jnp.float32)
            inner = (s * intra_ref[h]).astype(jnp.bfloat16)
        qd = (q.astype(jnp.float32) * qdec_ref[h]).astype(jnp.bfloat16)
        y = jnp.dot(qd, state.astype(jnp.bfloat16), preferred_element_type=jnp.float32)
        if not reverse:
            y = jnp.dot(inner, v, preferred_element_type=jnp.float32) + y
        kd = (k.astype(jnp.float32) * kdec_ref[h]).astype(jnp.bfloat16)
        state_ref[h] = state * cdec_ref[h] + lax.dot_general(
            kd, v, (((0,), (0,)), ((), ())), preferred_element_type=jnp.float32)
        if reverse:
            y = y + yf_ref[:, h * dv:(h + 1) * dv].astype(jnp.float32)
            y = y * lax.rsqrt(jnp.mean(y * y, axis=-1, keepdims=True) + RMS_EPS)
            g = g_ref[:, h * dv:(h + 1) * dv].astype(jnp.float32)
            y = g * jax.nn.sigmoid(g) * y
        o_ref[:, h * dv:(h + 1) * dv] = y.astype(o_ref.dtype)


def _decay_tables(log_gamma, reverse):
    c = RET_CHUNK
    lg = log_gamma.astype(jnp.float32)[:, None, None]
    pos = jnp.arange(c, dtype=jnp.float32)
    rel = pos[:, None] - pos[None, :]
    scale = RET_QK_DIM ** -0.5
    if reverse:
        intra = jnp.where(rel < 0, jnp.exp(lg * jnp.maximum(-rel, 0.0)[None]), 0.0)
        qdec = jnp.exp(lg * (c - pos)[None, :, None])
        kdec = jnp.exp(lg * pos[None, :, None])
    else:
        intra = jnp.where(rel >= 0, jnp.exp(lg * jnp.maximum(rel, 0.0)[None]), 0.0)
        qdec = jnp.exp(lg * (pos + 1.0)[None, :, None])
        kdec = jnp.exp(lg * (c - 1.0 - pos)[None, :, None])
    return intra * scale, qdec, kdec * scale, jnp.exp(lg * c)


def retention(proj, seq_lens, decay, reverse, y_fwd=None, decay_reverse=None):
    t = proj.shape[0]
    c = RET_CHUNK
    nc = t // c
    bounds = np.cumsum([0] + [s // c for s in seq_lens])
    reset = np.zeros((nc,), np.int32)
    if reverse:
        reset[nc - bounds[1:]] = 1
        chunk = lambda i, r: nc - 1 - i
    else:
        reset[bounds[:-1]] = 1
        chunk = lambda i, r: i
    log_gamma = lambda dec: -jnp.exp(dec.astype(jnp.float32))
    intra, qdec, kdec, cdec = _decay_tables(log_gamma(decay), reverse)
    d, vw, h = D_MODEL, RET_V_WIDTH, RET_HEADS
    const = lambda shape: pl.BlockSpec(shape, lambda i, r: (0,) * len(shape))
    in_specs = [pl.BlockSpec((c, d), lambda i, r: (chunk(i, r), 0)),
                pl.BlockSpec((c, d), lambda i, r: (chunk(i, r), 1)),
                pl.BlockSpec((c, vw), lambda i, r: (chunk(i, r), 1)),
                const((h, c, 1)), const((h, c, 1)), const((h, 1, 1))]
    args = [jnp.asarray(reset), proj, proj, proj, qdec, kdec, cdec]
    if reverse:
        in_specs += [pl.BlockSpec((c, vw), lambda i, r: (chunk(i, r), 0)),
                     pl.BlockSpec((c, vw), lambda i, r: (chunk(i, r), 2))]
        args += [y_fwd, proj]
    else:
        in_specs += [const((h, c, c))]
        args += [intra + _decay_tables(log_gamma(decay_reverse), True)[0]]
    grid_spec = pltpu.PrefetchScalarGridSpec(
        num_scalar_prefetch=1, grid=(nc,), in_specs=in_specs,
        out_specs=pl.BlockSpec((c, vw), lambda i, r: (chunk(i, r), 0)),
        scratch_shapes=[pltpu.VMEM((h, RET_QK_DIM, RET_V_DIM), jnp.float32)])
    return pl.pallas_call(
        functools.partial(_retention_kernel, reverse=reverse),
        grid_spec=grid_spec,
        out_shape=jax.ShapeDtypeStruct((t, vw), jnp.bfloat16),
        compiler_params=_cparams("arbitrary"),
        name="retention_bwd" if reverse else "retention_fwd",
    )(*args)


ATT_BLOCK = 2048
ATT_HALO = 1024
ATT_HALF = 64
ATT_Q = 128
ATT_DEINT = 4
assert tuple(d for _, d in DIL_PATTERNS) == (1, ATT_DEINT, ATT_DEINT * ATT_DEINT)
assert all(w == 2 * ATT_HALF * d for w, d in DIL_PATTERNS)
assert ATT_HALO == ATT_HALF * DIL_PATTERNS[-1][1] and ATT_BLOCK % ATT_HALO == 0


def _attention_kernel(vprev_ref, vnext_ref, q_ref, kp_ref, kc_ref, kn_ref, vp_ref, vc_ref, vn_ref,
                      bias_ref, o_ref, q4_ref, k4_ref, v4_ref, kedge_ref, vedge_ref, edge_ref,
                      num1_ref, m1_ref, l1_ref, num4_ref, m4_ref, l4_ref, onat_ref,
                      q16_ref, k16_ref, v16_ref, num16_ref, m16_ref, l16_ref):
    i = pl.program_id(0)
    b = ATT_BLOCK
    g = ATT_DEINT
    bq = b // g
    hq = ATT_HALO // g
    h = ATT_HALF
    half_lane = ATT_HEAD_DIM
    no_prev = vprev_ref[i] == 0
    no_next = vnext_ref[i] == 0
    scale = ATT_HEAD_DIM ** -0.5 * math.log2(math.e)
    kv_blocks = ((kp_ref, vp_ref), (kc_ref, vc_ref), (kn_ref, vn_ref))

    for c in range(g):
        q4_ref[c] = q_ref[pl.ds(c, bq, stride=g), :]
        first = 0
        for (k_blk, v_blk), n in zip(kv_blocks, (hq, bq, hq)):
            k4_ref[c, first:first + n] = k_blk[pl.ds(c, n, stride=g), :]
            v4_ref[c, first:first + n] = v_blk[pl.ds(c, n, stride=g), :]
            first += n

    nk1 = ATT_Q + 2 * h
    for edge_buf, (prv, cur, nxt) in ((kedge_ref, (kp_ref, kc_ref, kn_ref)),
                                      (vedge_ref, (vp_ref, vc_ref, vn_ref))):
        edge_buf[0, 0:h] = prv[ATT_HALO - h:ATT_HALO]
        edge_buf[0, h:nk1] = cur[0:nk1 - h]
        edge_buf[1, 0:nk1 - h] = cur[b - (nk1 - h):b]
        edge_buf[1, nk1 - h:nk1] = nxt[0:h]

    for br, (_, d) in enumerate(DIL_PATTERNS):
        nq = min(ATT_Q, b // d)
        nk = nq + 2 * h
        col = lax.broadcasted_iota(jnp.int32, (1, nk), 1)
        before = jnp.where((col < h) & no_prev, NEG_INF, 0.0)
        after = jnp.where((col >= nk - h) & no_next, NEG_INF, 0.0)
        if b // d == nq:
            before = before + after
        for hd in range(2):
            edge_ref[br, 0, hd, 0:nq, 0:nk] = bias_ref[br, hd, 0:nq, 0:nk] + before
            edge_ref[br, 1, hd, 0:nq, 0:nk] = bias_ref[br, hd, 0:nq, 0:nk] + after

    def bias_of(br, j, nsub, nq, nk):
        if j == 0:
            return [edge_ref[br, 0, hd, 0:nq, 0:nk] for hd in range(2)]
        if j == nsub - 1:
            return [edge_ref[br, 1, hd, 0:nq, 0:nk] for hd in range(2)]
        return [bias_ref[br, hd, 0:nq, 0:nk] for hd in range(2)]

    def scores(load_q, load_k, bias):
        q = load_q()
        nq = q.shape[0]
        first = lax.broadcasted_iota(jnp.int32, (nq, LANES), 1) < half_lane
        q = q * scale
        q2 = jnp.concatenate([jnp.where(first, q, 0.0), jnp.where(first, 0.0, q)], axis=0)
        s = lax.dot_general(q2.astype(jnp.bfloat16), load_k().astype(jnp.bfloat16),
                            (((1,), (1,)), ((), ())), preferred_element_type=jnp.float32)
        return s + jnp.concatenate(bias(), axis=0)

    def softmax(s):
        m = jnp.max(s, axis=-1, keepdims=True)
        p = jnp.exp2(s - m)
        return p.astype(jnp.bfloat16), m, jnp.sum(p, axis=-1, keepdims=True)

    def values(p, m, l, load_v, store):
        nq = p.shape[0] // 2
        first = lax.broadcasted_iota(jnp.int32, (nq, LANES), 1) < half_lane
        pv = jnp.dot(p, load_v().astype(jnp.bfloat16), preferred_element_type=jnp.float32)
        store(jnp.where(first, pv[:nq], pv[nq:]), jnp.where(first, m[:nq], m[nq:]),
              jnp.where(first, l[:nq], l[nq:]))

    units = []

    def store_to(num_r, m_r, l_r, idx):
        def store(num, m, l):
            num_r[idx] = num
            m_r[idx] = m
            l_r[idx] = l
        return store

    nq, nk, nsub = ATT_Q, nk1, b // ATT_Q
    for j in range(nsub):
        rows = slice(j * nq, (j + 1) * nq)
        keys = slice(j * nq - h, j * nq - h + nk)
        if j == 0:
            load_k, load_v = (lambda: kedge_ref[0]), (lambda: vedge_ref[0])
        elif j == nsub - 1:
            load_k, load_v = (lambda: kedge_ref[1]), (lambda: vedge_ref[1])
        else:
            load_k, load_v = (lambda keys=keys: kc_ref[keys]), (lambda keys=keys: vc_ref[keys])
        units.append(((lambda rows=rows: q_ref[rows]), load_k, load_v,
                      (lambda j=j, a=(nsub, nq, nk): bias_of(0, j, *a)),
                      store_to(num1_ref, m1_ref, l1_ref, rows)))

    nq = min(ATT_Q, bq)
    nk, nsub = nq + 2 * h, bq // nq
    for c in range(g):
        for j in range(nsub):
            rows = slice(j * nq, (j + 1) * nq)
            keys = slice(hq + j * nq - h, hq + j * nq - h + nk)
            units.append(((lambda c=c, rows=rows: q4_ref[c, rows]),
                          (lambda c=c, keys=keys: k4_ref[c, keys]),
                          (lambda c=c, keys=keys: v4_ref[c, keys]),
                          (lambda j=j, a=(nsub, nq, nk): bias_of(1, j, *a)),
                          store_to(num4_ref, m4_ref, l4_ref, (0, c, rows))))

    nq = bq // g
    nk = nq + 2 * h
    for c in range(g):
        for a in range(g):
            q16_ref[c, a] = q4_ref[c, pl.ds(a, nq, stride=g)]
            k16_ref[c, a] = k4_ref[c, pl.ds(a, nk, stride=g)]
            v16_ref[c, a] = v4_ref[c, pl.ds(a, nk, stride=g)]
            units.append(((lambda c=c, a=a: q16_ref[c, a]), (lambda c=c, a=a: k16_ref[c, a]),
                          (lambda c=c, a=a: v16_ref[c, a]),
                          (lambda a_=(1, nq, nk): bias_of(2, 0, *a_)),
                          store_to(num16_ref, m16_ref, l16_ref, (c, a))))

    s_prev = None
    sm_prev = None
    for t in range(len(units) + 2):
        s_new = scores(units[t][0], units[t][1], units[t][3]) if t < len(units) else None
        sm_new = softmax(s_prev) if s_prev is not None else None
        if sm_prev is not None:
            values(*sm_prev, units[t - 2][2], units[t - 2][4])
        s_prev, sm_prev = s_new, sm_new

    for c in range(g):
        for a in range(g):
            rows = pl.ds(a, nq, stride=g)
            num4_ref[1, c, rows] = num16_ref[c, a]
            m4_ref[1, c, rows] = m16_ref[c, a]
            l4_ref[1, c, rows] = l16_ref[c, a]

    for c in range(g):
        rows = pl.ds(c, bq, stride=g)
        ms = [m1_ref[rows], m4_ref[0, c], m4_ref[1, c]]
        ls = [l1_ref[rows], l4_ref[0, c], l4_ref[1, c]]
        nums = [num1_ref[rows], num4_ref[0, c], num4_ref[1, c]]
        m_max = jnp.maximum(jnp.maximum(ms[0], ms[1]), ms[2])
        wts = [jnp.exp2(mm - m_max) for mm in ms]
        den = wts[0] * ls[0] + wts[1] * ls[1] + wts[2] * ls[2]
        num = wts[0] * nums[0] + wts[1] * nums[1] + wts[2] * nums[2]
        onat_ref[rows] = num / den
    o_ref[...] = onat_ref[...].astype(o_ref.dtype)


def _alibi_bias():
    slopes = jnp.exp2(-8.0 * jnp.arange(1, ATT_HEADS + 1, dtype=jnp.float32) / ATT_HEADS)
    qi = jnp.arange(ATT_Q)
    ki = jnp.arange(ATT_Q + 2 * ATT_HALF) - ATT_HALF
    rel = jnp.abs(ki[None, :] - qi[:, None])
    dil = jnp.asarray([d for _, d in DIL_PATTERNS], jnp.float32)
    bias = -slopes[:, None, None, None] * (dil[None, :, None, None] * rel.astype(jnp.float32)[None, None])
    bias = jnp.where((rel <= ATT_HALF)[None, None], bias * math.log2(math.e), NEG_INF)
    return bias.reshape(ATT_HEADS // 2, 2, len(DIL_PATTERNS), *rel.shape).transpose(0, 2, 1, 3, 4)


def dilated_attention(qkv, seq_lens):
    t = qkv.shape[0]
    b = ATT_BLOCK
    g = ATT_DEINT
    nb = t // b
    assert all(s % b == 0 for s in seq_lens)
    bounds = np.cumsum([0] + [s // b for s in seq_lens])
    vprev = np.ones((nb,), np.int32)
    vnext = np.ones((nb,), np.int32)
    vprev[bounds[:-1]] = 0
    vnext[bounds[1:] - 1] = 0
    pairs = ATT_HEADS // 2
    cur = lambda off: pl.BlockSpec((b, LANES), lambda i, hp, vp, vn: (i, off + hp))
    ratio = b // ATT_HALO
    halo = lambda side, off: pl.BlockSpec(
        (ATT_HALO, LANES),
        lambda i, hp, vp, vn: (jnp.clip(i * ratio + (ratio if side > 0 else -1), 0, nb * ratio - 1), off + hp))
    bias = _alibi_bias()
    grid_spec = pltpu.PrefetchScalarGridSpec(
        num_scalar_prefetch=2, grid=(nb, pairs),
        in_specs=[cur(0),
                  halo(-1, pairs), cur(pairs), halo(1, pairs),
                  halo(-1, 2 * pairs), cur(2 * pairs), halo(1, 2 * pairs),
                  pl.BlockSpec((None,) + bias.shape[1:], lambda i, hp, vp, vn: (hp, 0, 0, 0, 0))],
        out_specs=pl.BlockSpec((b, LANES), lambda i, hp, vp, vn: (i, hp)),
        scratch_shapes=[pltpu.VMEM((g, b // g, LANES), jnp.float32),
                        pltpu.VMEM((g, (b + 2 * ATT_HALO) // g, LANES), jnp.float32),
                        pltpu.VMEM((g, (b + 2 * ATT_HALO) // g, LANES), jnp.float32),
                        pltpu.VMEM((2, ATT_Q + 2 * ATT_HALF, LANES), jnp.float32),
                        pltpu.VMEM((2, ATT_Q + 2 * ATT_HALF, LANES), jnp.float32),
                        pltpu.VMEM((len(DIL_PATTERNS), 2) + bias.shape[2:], jnp.float32)]
        + [pltpu.VMEM((b, LANES), jnp.float32)] * 3
        + [pltpu.VMEM((2, g, b // g, LANES), jnp.float32)] * 3
        + [pltpu.VMEM((b, LANES), jnp.float32)]
        + [pltpu.VMEM((g, g, b // (g * g), LANES), jnp.float32),
           pltpu.VMEM((g, g, b // (g * g) + 2 * ATT_HALF, LANES), jnp.float32),
           pltpu.VMEM((g, g, b // (g * g) + 2 * ATT_HALF, LANES), jnp.float32)]
        + [pltpu.VMEM((g, g, b // (g * g), LANES), jnp.float32)] * 3)
    return pl.pallas_call(
        _attention_kernel,
        grid_spec=grid_spec,
        out_shape=jax.ShapeDtypeStruct((t, D_MODEL), jnp.bfloat16),
        compiler_params=_cparams("parallel", "arbitrary"),
        name="dilated_attention",
    )(jnp.asarray(vprev), jnp.asarray(vnext), qkv, qkv, qkv, qkv, qkv, qkv, qkv, bias)


PROJ_COLS = 1024


def kernel(x_prompt, x_sample, norm_mix, norm_ffn, norm_final, ret_w_in, ret_w_out, ret_decay_fwd,
           ret_decay_bwd, att_w_qkv, att_w_out, moe_router, moe_w_gate, moe_w_up, moe_w_down):
    d = x_prompt.shape[-1]
    depth = norm_mix.shape[0]
    tp = x_prompt.shape[0] * x_prompt.shape[1]
    ts = x_sample.shape[0] * x_sample.shape[1]
    seq_lens = [x_prompt.shape[1]] * x_prompt.shape[0] + [x_sample.shape[1]] * x_sample.shape[0]
    groups = ((0, tp), (tp, ts))
    bf16 = lambda w: w.astype(jnp.bfloat16)

    x = (x_prompt.reshape(tp, d), x_sample.reshape(ts, d))
    for i in range(depth):
        j = i // 2
        last = i == depth - 1
        if i % 2 == 0:
            proj = norm_matmul(x, norm_mix[i], bf16(ret_w_in[j]), jnp.bfloat16, PROJ_COLS)
            y_fwd = retention(proj, seq_lens, ret_decay_fwd[j], False, decay_reverse=ret_decay_bwd[j])
            z = retention(proj, seq_lens, ret_decay_bwd[j], True, y_fwd)
            w_out = ret_w_out[j]
        else:
            qkv = norm_matmul(x, norm_mix[i], bf16(att_w_qkv[j]), jnp.float32, PROJ_COLS)
            z = dilated_attention(qkv, seq_lens)
            w_out = att_w_out[j]
        x1, hn, aff = outproj_router(z, bf16(w_out), x, norm_ffn[i], moe_router[i])
        x = moe_layer(x1, hn, aff, groups, moe_w_gate, moe_w_up, moe_w_down, i,
                      norm_final, final=last, out_rows=(tp, ts) if last else (tp + ts,))
    return x[0].reshape(x_prompt.shape), x[1].reshape(x_sample.shape)
```

```python
import functools
import math

import jax
import jax.numpy as jnp
import numpy as np
from jax import lax
from jax.experimental import pallas as pl
from jax.experimental.pallas import tpu as pltpu

D_MODEL = 1024
RET_HEADS = 4
RET_QK_DIM = D_MODEL // RET_HEADS
RET_V_WIDTH = 2 * D_MODEL
RET_V_DIM = RET_V_WIDTH // RET_HEADS
ATT_HEADS = 16
ATT_HEAD_DIM = D_MODEL // ATT_HEADS
DIL_PATTERNS = ((128, 1), (512, 4), (2048, 16))
N_EXPERTS = 16
EC_CAPACITY_FACTOR = 2
RMS_EPS = 1e-6
NEG_INF = -1e30

LANES = 128
BF16_ROWS = 16
VMEM_LIMIT = 56 * 1024 * 1024

ROW_TILE = 512
MOE_TILE = 256
MOE_WIN = 64
MOE_MAX_PASSES = MOE_TILE // MOE_WIN + 1
FFN_ROWS = 512
FFN_CHUNK = 256


def _cparams(*sem):
    return pltpu.CompilerParams(dimension_semantics=sem, vmem_limit_bytes=VMEM_LIMIT)


def _part_layout(parts, tile):
    specs, firsts, first = [], [], 0
    for p in parts:
        n = p.shape[0] // tile
        specs.append(pl.BlockSpec((tile, p.shape[1]),
                                  lambda i, *_, first=first, n=n: (jnp.clip(i - first, 0, n - 1), 0)))
        firsts.append(first)
        first += n
    return specs, tuple(firsts), first


def _read_parts(i, refs, firsts):
    x = refs[0][...]
    for ref, first in zip(refs[1:], firsts[1:]):
        x = jnp.where(i >= first, ref[...], x)
    return x


def _norm_matmul_kernel(*refs, tn, firsts):
    x_refs, (g_ref, w_ref, o_ref) = refs[:len(firsts)], refs[len(firsts):]
    x = _read_parts(pl.program_id(0), x_refs, firsts)
    ms = jnp.mean(x * x, axis=-1, keepdims=True)
    hn = (x * lax.rsqrt(ms + RMS_EPS) * g_ref[...]).astype(jnp.bfloat16)
    for c in range(w_ref.shape[1] // tn):
        cols = slice(c * tn, (c + 1) * tn)
        o_ref[:, cols] = jnp.dot(hn, w_ref[:, cols],
                                 preferred_element_type=jnp.float32).astype(o_ref.dtype)


def norm_matmul(x_parts, g, w_bf16, out_dtype, tn):
    d, n = w_bf16.shape
    x_specs, firsts, tiles = _part_layout(x_parts, ROW_TILE)
    return pl.pallas_call(
        functools.partial(_norm_matmul_kernel, tn=tn, firsts=firsts),
        grid=(tiles,),
        in_specs=x_specs + [pl.BlockSpec((1, d), lambda i: (0, 0)),
                            pl.BlockSpec((d, n), lambda i: (0, 0), pipeline_mode=pl.Buffered(1))],
        out_specs=pl.BlockSpec((ROW_TILE, n), lambda i: (i, 0)),
        out_shape=jax.ShapeDtypeStruct((tiles * ROW_TILE, n), out_dtype),
        compiler_params=_cparams("parallel"),
        name="norm_matmul",
    )(*x_parts, g.reshape(1, d), w_bf16)


def _outproj_router_kernel(z_ref, w_ref, g_ref, wr_ref, *refs, firsts):
    x_refs, (x1_ref, hn_ref, aff_ref) = refs[:len(firsts)], refs[len(firsts):]
    x = _read_parts(pl.program_id(0), x_refs, firsts)
    x1 = x + jnp.dot(z_ref[...], w_ref[...], preferred_element_type=jnp.float32)
    x1_ref[...] = x1
    ms = jnp.mean(x1 * x1, axis=-1, keepdims=True)
    hn = x1 * lax.rsqrt(ms + RMS_EPS) * g_ref[...]
    hn_hi = hn.astype(jnp.bfloat16)
    hn_ref[...] = hn_hi
    hn_lo = (hn - hn_hi.astype(jnp.float32)).astype(jnp.bfloat16)
    wr = wr_ref[...]
    wr_hi = wr.astype(jnp.bfloat16)
    wr_lo = (wr - wr_hi.astype(jnp.float32)).astype(jnp.bfloat16)
    nt = (((1,), (1,)), ((), ()))
    e = wr.shape[0]
    by_hi = lax.dot_general(jnp.concatenate([wr_hi, wr_lo], axis=0), hn_hi, nt,
                            preferred_element_type=jnp.float32)
    logits = (by_hi[:e] + by_hi[e:]
              + lax.dot_general(wr_hi, hn_lo, nt, preferred_element_type=jnp.float32))
    m = jnp.max(logits, axis=0, keepdims=True)
    p = jnp.exp(logits - m)
    aff_ref[...] = p / jnp.sum(p, axis=0, keepdims=True)


def outproj_router(z, w_bf16, x_parts, g, w_router):
    t, k = z.shape
    d, e = w_router.shape
    x_specs, firsts, tiles = _part_layout(x_parts, ROW_TILE)
    assert tiles * ROW_TILE == t
    return pl.pallas_call(
        functools.partial(_outproj_router_kernel, firsts=firsts),
        grid=(tiles,),
        in_specs=[pl.BlockSpec((ROW_TILE, k), lambda i: (i, 0)),
                  pl.BlockSpec((k, d), lambda i: (0, 0)),
                  pl.BlockSpec((1, d), lambda i: (0, 0)),
                  pl.BlockSpec((e, d), lambda i: (0, 0))] + x_specs,
        out_specs=[pl.BlockSpec((ROW_TILE, d), lambda i: (i, 0)),
                   pl.BlockSpec((ROW_TILE, d), lambda i: (i, 0)),
                   pl.BlockSpec((e, ROW_TILE), lambda i: (0, i))],
        out_shape=[jax.ShapeDtypeStruct((t, d), jnp.float32),
                   jax.ShapeDtypeStruct((t, d), jnp.bfloat16),
                   jax.ShapeDtypeStruct((e, t), jnp.float32)],
        compiler_params=_cparams("parallel"),
        name="outproj_router",
    )(z, w_bf16, g.reshape(1, d), w_router.T, *x_parts)


def _select_kernel(aff_ref, slot_ref, rowstart_ref, *, cap, base):
    e, r, _ = aff_ref.shape
    bits = pltpu.bitcast(aff_ref[...], jnp.int32)

    def count(mask):
        c = jnp.sum(jnp.where(mask, 1.0, 0.0), axis=2, keepdims=True)
        return jnp.sum(c, axis=1, keepdims=True)

    def search(i, thr):
        cand = thr | jnp.left_shift(jnp.int32(1), 30 - i)
        return jnp.where(count(bits >= cand) >= cap, cand, thr)

    thr = lax.fori_loop(0, 31, search, jnp.zeros((e, 1, 1), jnp.int32))
    gt = bits > thr
    eq = bits == thr
    need = cap - count(gt)

    row_i = lax.broadcasted_iota(jnp.int32, (LANES, LANES), 0)
    col_i = lax.broadcasted_iota(jnp.int32, (LANES, LANES), 1)
    upper = jnp.where(row_i <= col_i, 1.0, 0.0).astype(jnp.bfloat16)
    ones = jnp.ones((LANES, LANES), jnp.bfloat16)
    rr = lax.broadcasted_iota(jnp.int32, (r, r), 0)
    rc = lax.broadcasted_iota(jnp.int32, (r, r), 1)
    lower = jnp.where(rc < rr, 1.0, 0.0).astype(jnp.bfloat16)

    def excl_cumsum(mask):
        m = jnp.where(mask, 1.0, 0.0).astype(jnp.bfloat16).reshape(e * r, LANES)
        incl = jnp.dot(m, upper, preferred_element_type=jnp.float32)
        tot = jnp.dot(m, ones, preferred_element_type=jnp.float32)
        offs = []
        for ee in range(e):
            t_e = tot[ee * r:(ee + 1) * r].astype(jnp.bfloat16)
            offs.append(jnp.dot(lower, t_e, preferred_element_type=jnp.float32))
        off = jnp.concatenate(offs, axis=0)
        excl = incl - m.astype(jnp.float32) + off
        return excl.reshape(e, r, LANES), off.reshape(e, r, LANES)

    eq_rank, _ = excl_cumsum(eq)
    sel = gt | (eq & (eq_rank < need))
    pos, off = excl_cumsum(sel)
    slot_ref[...] = jnp.where(sel, pos.astype(jnp.int32) + base, -1)
    rowstart_ref[...] = off.astype(jnp.int32) + base


def select_tokens(aff3, cap, base):
    e, r, _ = aff3.shape
    return pl.pallas_call(
        functools.partial(_select_kernel, cap=cap, base=base),
        out_shape=[jax.ShapeDtypeStruct((e, r, LANES), jnp.int32),
                   jax.ShapeDtypeStruct((e, r, LANES), jnp.int32)],
        compiler_params=pltpu.CompilerParams(vmem_limit_bytes=VMEM_LIMIT),
        name="select_tokens",
    )(aff3)


def _dispatch_kernel(starts_ref, slot_ref, hn_ref, gp_ref, xg_ref, win_ref, carry_ref, sem_ref):
    i = pl.program_id(0)
    nt = pl.num_programs(0)
    e = slot_ref.shape[0]
    tt = slot_ref.shape[1]
    w = MOE_WIN
    buf = i % 2

    def aligned(s):
        return (s // BF16_ROWS) * BF16_ROWS

    a = [aligned(starts_ref[i * e + ee]) for ee in range(e)]
    end = [starts_ref[(i + 1) * e + ee] for ee in range(e)]
    n_pass = jnp.int32(1)
    for ee in range(e):
        n_pass = jnp.maximum(n_pass, (end[ee] - a[ee] + (w - 1)) // w)

    @pl.when(i == 0)
    def _():
        carry_ref[...] = jnp.zeros_like(carry_ref)
        win_ref[1, 0] = jnp.zeros(win_ref.shape[2:], win_ref.dtype)
        tail = [pltpu.make_async_copy(win_ref.at[1, 0], xg_ref.at[ee, pl.ds(r0, w), :], sem_ref.at[1, ee])
                for ee in range(e) for r0 in range(xg_ref.shape[1] - MOE_MAX_PASSES * w, xg_ref.shape[1], w)]
        for c in tail:
            c.start()
        for c in tail:
            c.wait()

    def copies(b, p):
        return [pltpu.make_async_copy(
            win_ref.at[b, ee],
            xg_ref.at[ee, pl.ds(pl.multiple_of(a[ee] + p * w, BF16_ROWS), w), :],
            sem_ref.at[b, ee]) for ee in range(e)]

    def wait_tile(b, src_i):
        for ee in range(e):
            pltpu.make_async_copy(win_ref.at[b, ee], xg_ref.at[ee, pl.ds(0, w), :],
                                  sem_ref.at[b, ee]).wait()

    hn = hn_ref[...]
    row = lax.broadcasted_iota(jnp.int32, (w, tt), 0)

    def one_pass(p, _):
        onehot = []
        for ee in range(e):
            rel = slot_ref[pl.ds(ee, 1), :] - (a[ee] + p * w)
            onehot.append(jnp.where(row == rel, 1.0, 0.0).astype(jnp.bfloat16))
        onehot = jnp.concatenate(onehot, axis=0)
        rows = jnp.concatenate([jnp.dot(onehot, hn, preferred_element_type=jnp.float32),
                                jnp.dot(onehot, gp_ref[...], preferred_element_type=jnp.float32)], axis=1)

        for ee in range(e):
            r_e = rows[ee * w:(ee + 1) * w]
            head = r_e[:BF16_ROWS] + jnp.where(p == 0, carry_ref[ee].astype(jnp.float32), 0.0)
            win_ref[buf, ee, pl.ds(0, BF16_ROWS), :] = head.astype(jnp.bfloat16)
            win_ref[buf, ee, pl.ds(BF16_ROWS, w - BF16_ROWS), :] = r_e[BF16_ROWS:].astype(jnp.bfloat16)

        for ee in range(e):
            nxt = aligned(end[ee]) - (a[ee] + p * w)
            held = win_ref[buf, ee, pl.ds(pl.multiple_of(jnp.clip(nxt, 0, w - BF16_ROWS), BF16_ROWS),
                                          BF16_ROWS), :]
            keep = (nxt >= 0) & (nxt < w)
            clear = (p == n_pass - 1) & (nxt >= w)
            carry_ref[ee] = jnp.where(keep, held, jnp.where(clear, jnp.zeros_like(held), carry_ref[ee]))

        @pl.when((p == 0) & (i > 0))
        def _():
            wait_tile(1 - buf, i - 1)

        for c in copies(buf, p):
            c.start()

        @pl.when(p + 1 < n_pass)
        def _():
            for c in copies(buf, p):
                c.wait()
        return 0

    lax.fori_loop(0, n_pass, one_pass, 0)

    @pl.when(i == nt - 1)
    def _():
        wait_tile(buf, i)


def dispatch(starts, slot, hn, gate_pieces, rows_padded):
    e, t = slot.shape
    d = hn.shape[1] + gate_pieces.shape[1]
    grid_spec = pltpu.PrefetchScalarGridSpec(
        num_scalar_prefetch=1,
        grid=(t // MOE_TILE,),
        in_specs=[pl.BlockSpec((e, MOE_TILE), lambda i, s: (0, i)),
                  pl.BlockSpec((MOE_TILE, hn.shape[1]), lambda i, s: (i, 0)),
                  pl.BlockSpec((MOE_TILE, gate_pieces.shape[1]), lambda i, s: (i, 0))],
        out_specs=pl.BlockSpec(memory_space=pl.ANY),
        scratch_shapes=[pltpu.VMEM((2, e, MOE_WIN, d), jnp.bfloat16),
                        pltpu.VMEM((e, BF16_ROWS, d), jnp.bfloat16),
                        pltpu.SemaphoreType.DMA((2, e))])
    return pl.pallas_call(
        _dispatch_kernel,
        grid_spec=grid_spec,
        out_shape=jax.ShapeDtypeStruct((e, rows_padded, d), jnp.bfloat16),
        compiler_params=_cparams("arbitrary"),
        name="moe_dispatch",
    )(starts, slot, hn, gate_pieces)


FFN_SLABS = 16


def _ffn_kernel(x_ref, wg_hbm, wu_hbm, wd_hbm, y_ref, wg_ref, wu_ref, wd_ref, sg_ref, su_ref, sd_ref,
                acc_ref, sem_ref, *, layer, steps):
    ee = pl.program_id(0)
    m = pl.program_id(1)
    d, f = acc_ref.shape[1], wg_ref.shape[2]
    in_rows, hid_rows = sg_ref.shape[0], sd_ref.shape[0]
    cur = ee % 2

    def slab_copies(expert, c):
        r_in = pl.ds(pl.multiple_of(c * in_rows, BF16_ROWS), in_rows)
        r_hid = pl.ds(pl.multiple_of(c * hid_rows, BF16_ROWS), hid_rows)
        return [pltpu.make_async_copy(wg_hbm.at[layer, expert, r_in, :], sg_ref, sem_ref.at[0]),
                pltpu.make_async_copy(wu_hbm.at[layer, expert, r_in, :], su_ref, sem_ref.at[1]),
                pltpu.make_async_copy(wd_hbm.at[layer, expert, r_hid, :], sd_ref, sem_ref.at[2])]

    def land(copies, half, c):
        for cp in copies:
            cp.wait()
        r_in = pl.ds(pl.multiple_of(c * in_rows, BF16_ROWS), in_rows)
        r_hid = pl.ds(pl.multiple_of(c * hid_rows, BF16_ROWS), hid_rows)
        wg_ref[half, r_in, :] = sg_ref[...].astype(jnp.bfloat16)
        wu_ref[half, r_in, :] = su_ref[...].astype(jnp.bfloat16)
        wd_ref[half, r_hid, :] = sd_ref[...].astype(jnp.bfloat16)

    @pl.when((ee == 0) & (m == 0))
    def _():
        def fetch(c, _):
            copies = slab_copies(0, c)
            for cp in copies:
                cp.start()
            land(copies, 0, c)
            return 0
        lax.fori_loop(0, FFN_SLABS, fetch, 0)

    per_step = -(-FFN_SLABS // steps)
    first = m * per_step
    has_next = ee + 1 < pl.num_programs(0)
    prefetch = has_next & (first < FFN_SLABS)
    src_expert = jnp.minimum(ee + 1, pl.num_programs(0) - 1)
    copies = slab_copies(src_expert, jnp.minimum(first, FFN_SLABS - 1))
    for cp in copies:
        cp.start()

    x = x_ref[:, 0:d]
    n_chunks = f // FFN_CHUNK
    for c in range(n_chunks):
        if c == n_chunks - 2:
            land(copies, 1 - cur, jnp.where(prefetch, first, FFN_SLABS))
        cols = slice(c * FFN_CHUNK, (c + 1) * FFN_CHUNK)
        g = jnp.dot(x, wg_ref[cur, 0:d, cols], preferred_element_type=jnp.float32)
        u = jnp.dot(x, wu_ref[cur, 0:d, cols], preferred_element_type=jnp.float32)
        h = (g * jax.nn.sigmoid(g) * u).astype(jnp.bfloat16)
        part = jnp.dot(h, wd_ref[cur, cols, :], preferred_element_type=jnp.float32)
        if c == 0:
            acc_ref[...] = part
        else:
            acc_ref[...] += part
    pieces = x_ref[:, d:].astype(jnp.float32)
    lane = lax.broadcasted_iota(jnp.int32, pieces.shape, 1)
    n_exp = pl.num_programs(0)
    mine = (lane == ee) | (lane == n_exp + ee) | (lane == 2 * n_exp + ee)
    gate = jnp.sum(jnp.where(mine, pieces, 0.0), axis=1, keepdims=True)
    y_ref[...] = (acc_ref[...] * gate).astype(y_ref.dtype)

    for k in range(1, per_step):
        @pl.when(has_next & (first + k < FFN_SLABS))
        def _():
            more = slab_copies(ee + 1, first + k)
            for cp in more:
                cp.start()
            land(more, 1 - cur, first + k)


def expert_ffn(xg, wg, wu, wd, layer, rows):
    e, _, dx = xg.shape
    d, f = wg.shape[2], wg.shape[3]
    in_rows, hid_rows = d // FFN_SLABS, f // FFN_SLABS
    assert in_rows % BF16_ROWS == 0 and hid_rows % BF16_ROWS == 0 and f % FFN_CHUNK == 0
    return pl.pallas_call(
        functools.partial(_ffn_kernel, layer=layer, steps=rows // FFN_ROWS),
        grid=(e, rows // FFN_ROWS),
        in_specs=[pl.BlockSpec((None, FFN_ROWS, dx), lambda ee, m: (ee, m, 0)),
                  pl.BlockSpec(memory_space=pl.ANY),
                  pl.BlockSpec(memory_space=pl.ANY),
                  pl.BlockSpec(memory_space=pl.ANY)],
        out_specs=pl.BlockSpec((None, FFN_ROWS, d), lambda ee, m: (ee, m, 0)),
        out_shape=jax.ShapeDtypeStruct((e, rows, d), jnp.bfloat16),
        scratch_shapes=[pltpu.VMEM((2, d + in_rows, f), jnp.bfloat16),
                        pltpu.VMEM((2, d + in_rows, f), jnp.bfloat16),
                        pltpu.VMEM((2, f + hid_rows, d), jnp.bfloat16),
                        pltpu.VMEM((in_rows, f), jnp.float32),
                        pltpu.VMEM((in_rows, f), jnp.float32),
                        pltpu.VMEM((hid_rows, d), jnp.float32),
                        pltpu.VMEM((FFN_ROWS, d), jnp.float32),
                        pltpu.SemaphoreType.DMA((3,))],
        compiler_params=_cparams("arbitrary", "arbitrary"),
        name="expert_ffn",
    )(xg, wg, wu, wd)


def _combine_kernel(starts_ref, slot_ref, x_ref, gain_ref, ys_ref, *refs, rows, final, out_tiles):
    o_refs, (buf_ref, sem_ref) = refs[:len(out_tiles)], refs[len(out_tiles):]
    i = pl.program_id(0)
    nt = pl.num_programs(0)
    tt, e = slot_ref.shape
    w = MOE_WIN
    d = x_ref.shape[1]
    b = i % 2

    def aligned(s):
        return (s // BF16_ROWS) * BF16_ROWS

    def window_start(ti, ee, p):
        return jnp.minimum(aligned(starts_ref[ti * e + ee]) + p * w, rows - w)

    def copies(ti, bb, p):
        return [pltpu.make_async_copy(
            ys_ref.at[ee, pl.ds(pl.multiple_of(window_start(ti, ee, p), BF16_ROWS), w), :],
            buf_ref.at[bb, pl.ds(ee * w, w), :],
            sem_ref.at[bb, ee]) for ee in range(e)]

    @pl.when(i == 0)
    def _():
        for c in copies(i, b, 0):
            c.start()

    @pl.when(i + 1 < nt)
    def _():
        for c in copies(i + 1, 1 - b, 0):
            c.start()

    n_pass = jnp.int32(1)
    for ee in range(e):
        n_pass = jnp.maximum(
            n_pass, (starts_ref[(i + 1) * e + ee] - aligned(starts_ref[i * e + ee]) + (w - 1)) // w)

    slot = slot_ref[...]
    expert = lax.broadcasted_iota(jnp.int32, (1, e), 1)
    spread = jnp.where(lax.broadcasted_iota(jnp.int32, (e, e * w), 1) // w
                       == lax.broadcasted_iota(jnp.int32, (e, e * w), 0), 1.0, 0.0).astype(jnp.bfloat16)
    col_in_window = (lax.broadcasted_iota(jnp.int32, (tt, e * w), 1) % w).astype(jnp.float32)

    def one_pass(p, acc):
        @pl.when(p > 0)
        def _():
            for c in copies(i, b, p):
                c.start()

        for c in copies(i, b, p):
            c.wait()

        lo = jnp.zeros((1, e), jnp.int32)
        ws = jnp.zeros((1, e), jnp.int32)
        for ee in range(e):
            lo = jnp.where(expert == ee, aligned(starts_ref[i * e + ee]) + p * w, lo)
            ws = jnp.where(expert == ee, window_start(i, ee, p), ws)
        ok = (slot >= lo) & (slot < lo + w)
        rel = jnp.where(ok, slot - ws, -1).astype(jnp.float32).astype(jnp.bfloat16)
        rel_cols = jnp.dot(rel, spread, preferred_element_type=jnp.float32)
        onehot = jnp.where(rel_cols == col_in_window, 1.0, 0.0).astype(jnp.bfloat16)
        return acc + jnp.dot(onehot, buf_ref[b], preferred_element_type=jnp.float32)

    moe = lax.fori_loop(0, n_pass, one_pass, jnp.zeros((tt, d), jnp.float32))
    x = x_ref[...] + moe
    if final:
        ms = jnp.mean(x * x, axis=-1, keepdims=True)
        x = x * lax.rsqrt(ms + RMS_EPS) * gain_ref[...]
    first = 0
    for o_ref, n in zip(o_refs, out_tiles):
        @pl.when((i >= first) & (i < first + n))
        def _(o_ref=o_ref):
            o_ref[...] = x
        first += n


def combine(starts, slot_t, x, gain, ys, final, out_rows):
    t, e = slot_t.shape
    d = x.shape[1]
    rows = ys.shape[1]
    out_tiles = tuple(r // MOE_TILE for r in out_rows)
    assert sum(out_rows) == t
    out_specs, first = [], 0
    for n in out_tiles:
        out_specs.append(pl.BlockSpec((MOE_TILE, d),
                                      lambda i, s, first=first, n=n: (jnp.clip(i - first, 0, n - 1), 0)))
        first += n
    grid_spec = pltpu.PrefetchScalarGridSpec(
        num_scalar_prefetch=1,
        grid=(t // MOE_TILE,),
        in_specs=[pl.BlockSpec((MOE_TILE, e), lambda i, s: (i, 0)),
                  pl.BlockSpec((MOE_TILE, d), lambda i, s: (i, 0)),
                  pl.BlockSpec((1, d), lambda i, s: (0, 0)),
                  pl.BlockSpec(memory_space=pl.ANY)],
        out_specs=out_specs,
        scratch_shapes=[pltpu.VMEM((2, e * MOE_WIN, d), jnp.bfloat16),
                        pltpu.SemaphoreType.DMA((2, e))])
    return pl.pallas_call(
        functools.partial(_combine_kernel, rows=rows, final=final, out_tiles=out_tiles),
        grid_spec=grid_spec,
        out_shape=[jax.ShapeDtypeStruct((r, d), jnp.float32) for r in out_rows],
        compiler_params=_cparams("arbitrary"),
        name="moe_combine",
    )(starts, slot_t, x, gain.reshape(1, d), ys)


def moe_layer(x1, hn, aff, groups, wg, wu, wd, layer, gain, final, out_rows):
    e, t = aff.shape
    slots, rowstarts = [], []
    base = 0
    for (t0, tg) in groups:
        cap = EC_CAPACITY_FACTOR * tg // e
        aff3 = aff[:, t0:t0 + tg].reshape(e, tg // LANES, LANES)
        s, r = select_tokens(aff3, cap, base)
        slots.append(s.reshape(e, tg))
        rowstarts.append(r[:, ::MOE_TILE // LANES, 0])
        base += cap
    rows = base
    slot = jnp.concatenate(slots, axis=1)
    starts = jnp.concatenate(rowstarts + [jnp.full((e, 1), rows, jnp.int32)], axis=1)
    starts = starts.T.reshape(-1)
    gate = aff.T
    g_hi = gate.astype(jnp.bfloat16)
    g_mid = (gate - g_hi.astype(jnp.float32)).astype(jnp.bfloat16)
    g_lo = (gate - g_hi.astype(jnp.float32) - g_mid.astype(jnp.float32)).astype(jnp.bfloat16)
    pieces = jnp.concatenate([g_hi, g_mid, g_lo, jnp.zeros((t, LANES - 3 * e), jnp.bfloat16)], axis=1)
    xg = dispatch(starts, slot, hn, pieces, rows + MOE_MAX_PASSES * MOE_WIN)
    ys = expert_ffn(xg, wg, wu, wd, layer, rows)
    return tuple(combine(starts, slot.T, x1, gain, ys, final, out_rows))


RET_CHUNK = 256


def _retention_kernel(reset_ref, q_ref, k_ref, v_ref, qdec_ref, kdec_ref, cdec_ref, *rest, reverse):
    if reverse:
        yf_ref, g_ref, o_ref, state_ref = rest
    else:
        intra_ref, o_ref, state_ref = rest
    i = pl.program_id(0)

    @pl.when(reset_ref[i] == 1)
    def _():
        state_ref[...] = jnp.zeros_like(state_ref)

    dk, dv = RET_QK_DIM, RET_V_DIM
    for h in range(RET_HEADS):
        q = q_ref[:, h * dk:(h + 1) * dk]
        k = k_ref[:, h * dk:(h + 1) * dk]
        v = v_ref[:, h * dv:(h + 1) * dv]
        state = state_ref[h]
        if not reverse:
            s = lax.dot_general(q, k, (((1,), (1,)), ((), ())), preferred_element_type=jnp.float32)
            inner = (s * intra_ref[h]).astype(jnp.bfloat16)
        qd = (q.astype(jnp.float32) * qdec_ref[h]).astype(jnp.bfloat16)
        y = jnp.dot(qd, state.astype(jnp.bfloat16), preferred_element_type=jnp.float32)
        if not reverse:
            y = jnp.dot(inner, v, preferred_element_type=jnp.float32) + y
        kd = (k.astype(jnp.float32) * kdec_ref[h]).astype(jnp.bfloat16)
        state_ref[h] = state * cdec_ref[h] + lax.dot_general(
            kd, v, (((0,), (0,)), ((), ())), preferred_element_type=jnp.float32)
        if reverse:
            y = y + yf_ref[:, h * dv:(h + 1) * dv].astype(jnp.float32)
            y = y * lax.rsqrt(jnp.mean(y * y, axis=-1, keepdims=True) + RMS_EPS)
            g = g_ref[:, h * dv:(h + 1) * dv].astype(jnp.float32)
            y = g * jax.nn.sigmoid(g) * y
        o_ref[:, h * dv:(h + 1) * dv] = y.astype(o_ref.dtype)


def _decay_tables(log_gamma, reverse):
    c = RET_CHUNK
    lg = log_gamma.astype(jnp.float32)[:, None, None]
    pos = jnp.arange(c, dtype=jnp.float32)
    rel = pos[:, None] - pos[None, :]
    scale = RET_QK_DIM ** -0.5
    if reverse:
        intra = jnp.where(rel < 0, jnp.exp(lg * jnp.maximum(-rel, 0.0)[None]), 0.0)
        qdec = jnp.exp(lg * (c - pos)[None, :, None])
        kdec = jnp.exp(lg * pos[None, :, None])
    else:
        intra = jnp.where(rel >= 0, jnp.exp(lg * jnp.maximum(rel, 0.0)[None]), 0.0)
        qdec = jnp.exp(lg * (pos + 1.0)[None, :, None])
        kdec = jnp.exp(lg * (c - 1.0 - pos)[None, :, None])
    return intra * scale, qdec, kdec * scale, jnp.exp(lg * c)


def retention(proj, seq_lens, decay, reverse, y_fwd=None, decay_reverse=None):
    t = proj.shape[0]
    c = RET_CHUNK
    nc = t // c
    bounds = np.cumsum([0] + [s // c for s in seq_lens])
    reset = np.zeros((nc,), np.int32)
    if reverse:
        reset[nc - bounds[1:]] = 1
        chunk = lambda i, r: nc - 1 - i
    else:
        reset[bounds[:-1]] = 1
        chunk = lambda i, r: i
    log_gamma = lambda dec: -jnp.exp(dec.astype(jnp.float32))
    intra, qdec, kdec, cdec = _decay_tables(log_gamma(decay), reverse)
    d, vw, h = D_MODEL, RET_V_WIDTH, RET_HEADS
    const = lambda shape: pl.BlockSpec(shape, lambda i, r: (0,) * len(shape))
    in_specs = [pl.BlockSpec((c, d), lambda i, r: (chunk(i, r), 0)),
                pl.BlockSpec((c, d), lambda i, r: (chunk(i, r), 1)),
                pl.BlockSpec((c, vw), lambda i, r: (chunk(i, r), 1)),
                const((h, c, 1)), const((h, c, 1)), const((h, 1, 1))]
    args = [jnp.asarray(reset), proj, proj, proj, qdec, kdec, cdec]
    if reverse:
        in_specs += [pl.BlockSpec((c, vw), lambda i, r: (chunk(i, r), 0)),
                     pl.BlockSpec((c, vw), lambda i, r: (chunk(i, r), 2))]
        args += [y_fwd, proj]
    else:
        in_specs += [const((h, c, c))]
        args += [intra + _decay_tables(log_gamma(decay_reverse), True)[0]]
    grid_spec = pltpu.PrefetchScalarGridSpec(
        num_scalar_prefetch=1, grid=(nc,), in_specs=in_specs,
        out_specs=pl.BlockSpec((c, vw), lambda i, r: (chunk(i, r), 0)),
        scratch_shapes=[pltpu.VMEM((h, RET_QK_DIM, RET_V_DIM), jnp.float32)])
    return pl.pallas_call(
        functools.partial(_retention_kernel, reverse=reverse),
        grid_spec=grid_spec,
        out_shape=jax.ShapeDtypeStruct((t, vw), jnp.bfloat16),
        compiler_params=_cparams("arbitrary"),
        name="retention_bwd" if reverse else "retention_fwd",
    )(*args)


ATT_BLOCK = 2048
ATT_HALO = 1024
ATT_HALF = 64
ATT_Q = 128
ATT_DEINT = 4
assert tuple(d for _, d in DIL_PATTERNS) == (1, ATT_DEINT, ATT_DEINT * ATT_DEINT)
assert all(w == 2 * ATT_HALF * d for w, d in DIL_PATTERNS)
assert ATT_HALO == ATT_HALF * DIL_PATTERNS[-1][1] and ATT_BLOCK % ATT_HALO == 0


def _attention_kernel(vprev_ref, vnext_ref, q_ref, kp_ref, kc_ref, kn_ref, vp_ref, vc_ref, vn_ref,
                      bias_ref, o_ref, q4_ref, k4_ref, v4_ref, kedge_ref, vedge_ref, edge_ref,
                      num1_ref, m1_ref, l1_ref, num4_ref, m4_ref, l4_ref, onat_ref,
                      q16_ref, k16_ref, v16_ref, num16_ref, m16_ref, l16_ref):
    i = pl.program_id(0)
    b = ATT_BLOCK
    g = ATT_DEINT
    bq = b // g
    hq = ATT_HALO // g
    h = ATT_HALF
    half_lane = ATT_HEAD_DIM
    no_prev = vprev_ref[i] == 0
    no_next = vnext_ref[i] == 0
    scale = ATT_HEAD_DIM ** -0.5 * math.log2(math.e)
    kv_blocks = ((kp_ref, vp_ref), (kc_ref, vc_ref), (kn_ref, vn_ref))

    for c in range(g):
        q4_ref[c] = q_ref[pl.ds(c, bq, stride=g), :]
        first = 0
        for (k_blk, v_blk), n in zip(kv_blocks, (hq, bq, hq)):
            k4_ref[c, first:first + n] = k_blk[pl.ds(c, n, stride=g), :]
            v4_ref[c, first:first + n] = v_blk[pl.ds(c, n, stride=g), :]
            first += n

    nk1 = ATT_Q + 2 * h
    for edge_buf, (prv, cur, nxt) in ((kedge_ref, (kp_ref, kc_ref, kn_ref)),
                                      (vedge_ref, (vp_ref, vc_ref, vn_ref))):
        edge_buf[0, 0:h] = prv[ATT_HALO - h:ATT_HALO]
        edge_buf[0, h:nk1] = cur[0:nk1 - h]
        edge_buf[1, 0:nk1 - h] = cur[b - (nk1 - h):b]
        edge_buf[1, nk1 - h:nk1] = nxt[0:h]

    for br, (_, d) in enumerate(DIL_PATTERNS):
        nq = min(ATT_Q, b // d)
        nk = nq + 2 * h
        col = lax.broadcasted_iota(jnp.int32, (1, nk), 1)
        before = jnp.where((col < h) & no_prev, NEG_INF, 0.0)
        after = jnp.where((col >= nk - h) & no_next, NEG_INF, 0.0)
        if b // d == nq:
            before = before + after
        for hd in range(2):
            edge_ref[br, 0, hd, 0:nq, 0:nk] = bias_ref[br, hd, 0:nq, 0:nk] + before
            edge_ref[br, 1, hd, 0:nq, 0:nk] = bias_ref[br, hd, 0:nq, 0:nk] + after

    def bias_of(br, j, nsub, nq, nk):
        if j == 0:
            return [edge_ref[br, 0, hd, 0:nq, 0:nk] for hd in range(2)]
        if j == nsub - 1:
            return [edge_ref[br, 1, hd, 0:nq, 0:nk] for hd in range(2)]
        return [bias_ref[br, hd, 0:nq, 0:nk] for hd in range(2)]

    def scores(load_q, load_k, bias):
        q = load_q()
        nq = q.shape[0]
        first = lax.broadcasted_iota(jnp.int32, (nq, LANES), 1) < half_lane
        q = q * scale
        q2 = jnp.concatenate([jnp.where(first, q, 0.0), jnp.where(first, 0.0, q)], axis=0)
        s = lax.dot_general(q2.astype(jnp.bfloat16), load_k().astype(jnp.bfloat16),
                            (((1,), (1,)), ((), ())), preferred_element_type=jnp.float32)
        return s + jnp.concatenate(bias(), axis=0)

    def softmax(s):
        m = jnp.max(s, axis=-1, keepdims=True)
        p = jnp.exp2(s - m)
        return p.astype(jnp.bfloat16), m, jnp.sum(p, axis=-1, keepdims=True)

    def values(p, m, l, load_v, store):
        nq = p.shape[0] // 2
        first = lax.broadcasted_iota(jnp.int32, (nq, LANES), 1) < half_lane
        pv = jnp.dot(p, load_v().astype(jnp.bfloat16), preferred_element_type=jnp.float32)
        store(jnp.where(first, pv[:nq], pv[nq:]), jnp.where(first, m[:nq], m[nq:]),
              jnp.where(first, l[:nq], l[nq:]))

    units = []

    def store_to(num_r, m_r, l_r, idx):
        def store(num, m, l):
            num_r[idx] = num
            m_r[idx] = m
            l_r[idx] = l
        return store

    nq, nk, nsub = ATT_Q, nk1, b // ATT_Q
    for j in range(nsub):
        rows = slice(j * nq, (j + 1) * nq)
        keys = slice(j * nq - h, j * nq - h + nk)
        if j == 0:
            load_k, load_v = (lambda: kedge_ref[0]), (lambda: vedge_ref[0])
        elif j == nsub - 1:
            load_k, load_v = (lambda: kedge_ref[1]), (lambda: vedge_ref[1])
        else:
            load_k, load_v = (lambda keys=keys: kc_ref[keys]), (lambda keys=keys: vc_ref[keys])
        units.append(((lambda rows=rows: q_ref[rows]), load_k, load_v,
                      (lambda j=j, a=(nsub, nq, nk): bias_of(0, j, *a)),
                      store_to(num1_ref, m1_ref, l1_ref, rows)))

    nq = min(ATT_Q, bq)
    nk, nsub = nq + 2 * h, bq // nq
    for c in range(g):
        for j in range(nsub):
            rows = slice(j * nq, (j + 1) * nq)
            keys = slice(hq + j * nq - h, hq + j * nq - h + nk)
            units.append(((lambda c=c, rows=rows: q4_ref[c, rows]),
                          (lambda c=c, keys=keys: k4_ref[c, keys]),
                          (lambda c=c, keys=keys: v4_ref[c, keys]),
                          (lambda j=j, a=(nsub, nq, nk): bias_of(1, j, *a)),
                          store_to(num4_ref, m4_ref, l4_ref, (0, c, rows))))

    nq = bq // g
    nk = nq + 2 * h
    for c in range(g):
        for a in range(g):
            q16_ref[c, a] = q4_ref[c, pl.ds(a, nq, stride=g)]
            k16_ref[c, a] = k4_ref[c, pl.ds(a, nk, stride=g)]
            v16_ref[c, a] = v4_ref[c, pl.ds(a, nk, stride=g)]
            units.append(((lambda c=c, a=a: q16_ref[c, a]), (lambda c=c, a=a: k16_ref[c, a]),
                          (lambda c=c, a=a: v16_ref[c, a]),
                          (lambda a_=(1, nq, nk): bias_of(2, 0, *a_)),
                          store_to(num16_ref, m16_ref, l16_ref, (c, a))))

    s_prev = None
    sm_prev = None
    for t in range(len(units) + 2):
        s_new = scores(units[t][0], units[t][1], units[t][3]) if t < len(units) else None
        sm_new = softmax(s_prev) if s_prev is not None else None
        if sm_prev is not None:
            values(*sm_prev, units[t - 2][2], units[t - 2][4])
        s_prev, sm_prev = s_new, sm_new

    for c in range(g):
        for a in range(g):
            rows = pl.ds(a, nq, stride=g)
            num4_ref[1, c, rows] = num16_ref[c, a]
            m4_ref[1, c, rows] = m16_ref[c, a]
            l4_ref[1, c, rows] = l16_ref[c, a]

    for c in range(g):
        rows = pl.ds(c, bq, stride=g)
        ms = [m1_ref[rows], m4_ref[0, c], m4_ref[1, c]]
        ls = [l1_ref[rows], l4_ref[0, c], l4_ref[1, c]]
        nums = [num1_ref[rows], num4_ref[0, c], num4_ref[1, c]]
        m_max = jnp.maximum(jnp.maximum(ms[0], ms[1]), ms[2])
        wts = [jnp.exp2(mm - m_max) for mm in ms]
        den = wts[0] * ls[0] + wts[1] * ls[1] + wts[2] * ls[2]
        num = wts[0] * nums[0] + wts[1] * nums[1] + wts[2] * nums[2]
        onat_ref[rows] = num / den
    o_ref[...] = onat_ref[...].astype(o_ref.dtype)


def _alibi_bias():
    slopes = jnp.exp2(-8.0 * jnp.arange(1, ATT_HEADS + 1, dtype=jnp.float32) / ATT_HEADS)
    qi = jnp.arange(ATT_Q)
    ki = jnp.arange(ATT_Q + 2 * ATT_HALF) - ATT_HALF
    rel = jnp.abs(ki[None, :] - qi[:, None])
    dil = jnp.asarray([d for _, d in DIL_PATTERNS], jnp.float32)
    bias = -slopes[:, None, None, None] * (dil[None, :, None, None] * rel.astype(jnp.float32)[None, None])
    bias = jnp.where((rel <= ATT_HALF)[None, None], bias * math.log2(math.e), NEG_INF)
    return bias.reshape(ATT_HEADS // 2, 2, len(DIL_PATTERNS), *rel.shape).transpose(0, 2, 1, 3, 4)


def dilated_attention(qkv, seq_lens):
    t = qkv.shape[0]
    b = ATT_BLOCK
    g = ATT_DEINT
    nb = t // b
    assert all(s % b == 0 for s in seq_lens)
    bounds = np.cumsum([0] + [s // b for s in seq_lens])
    vprev = np.ones((nb,), np.int32)
    vnext = np.ones((nb,), np.int32)
    vprev[bounds[:-1]] = 0
    vnext[bounds[1:] - 1] = 0
    pairs = ATT_HEADS // 2
    cur = lambda off: pl.BlockSpec((b, LANES), lambda i, hp, vp, vn: (i, off + hp))
    ratio = b // ATT_HALO
    halo = lambda side, off: pl.BlockSpec(
        (ATT_HALO, LANES),
        lambda i, hp, vp, vn: (jnp.clip(i * ratio + (ratio if side > 0 else -1), 0, nb * ratio - 1), off + hp))
    bias = _alibi_bias()
    grid_spec = pltpu.PrefetchScalarGridSpec(
        num_scalar_prefetch=2, grid=(nb, pairs),
        in_specs=[cur(0),
                  halo(-1, pairs), cur(pairs), halo(1, pairs),
                  halo(-1, 2 * pairs), cur(2 * pairs), halo(1, 2 * pairs),
                  pl.BlockSpec((None,) + bias.shape[1:], lambda i, hp, vp, vn: (hp, 0, 0, 0, 0))],
        out_specs=pl.BlockSpec((b, LANES), lambda i, hp, vp, vn: (i, hp)),
        scratch_shapes=[pltpu.VMEM((g, b // g, LANES), jnp.float32),
                        pltpu.VMEM((g, (b + 2 * ATT_HALO) // g, LANES), jnp.float32),
                        pltpu.VMEM((g, (b + 2 * ATT_HALO) // g, LANES), jnp.float32),
                        pltpu.VMEM((2, ATT_Q + 2 * ATT_HALF, LANES), jnp.float32),
                        pltpu.VMEM((2, ATT_Q + 2 * ATT_HALF, LANES), jnp.float32),
                        pltpu.VMEM((len(DIL_PATTERNS), 2) + bias.shape[2:], jnp.float32)]
        + [pltpu.VMEM((b, LANES), jnp.float32)] * 3
        + [pltpu.VMEM((2, g, b // g, LANES), jnp.float32)] * 3
        + [pltpu.VMEM((b, LANES), jnp.float32)]
        + [pltpu.VMEM((g, g, b // (g * g), LANES), jnp.float32),
           pltpu.VMEM((g, g, b // (g * g) + 2 * ATT_HALF, LANES), jnp.float32),
           pltpu.VMEM((g, g, b // (g * g) + 2 * ATT_HALF, LANES), jnp.float32)]
        + [pltpu.VMEM((g, g, b // (g * g), LANES), jnp.float32)] * 3)
    return pl.pallas_call(
        _attention_kernel,
        grid_spec=grid_spec,
        out_shape=jax.ShapeDtypeStruct((t, D_MODEL), jnp.bfloat16),
        compiler_params=_cparams("parallel", "arbitrary"),
        name="dilated_attention",
    )(jnp.asarray(vprev), jnp.asarray(vnext), qkv, qkv, qkv, qkv, qkv, qkv, qkv, bias)


PROJ_COLS = 1024


def kernel(x_prompt, x_sample, norm_mix, norm_ffn, norm_final, ret_w_in, ret_w_out, ret_decay_fwd,
           ret_decay_bwd, att_w_qkv, att_w_out, moe_router, moe_w_gate, moe_w_up, moe_w_down):
    d = x_prompt.shape[-1]
    depth = norm_mix.shape[0]
    tp = x_prompt.shape[0] * x_prompt.shape[1]
    ts = x_sample.shape[0] * x_sample.shape[1]
    seq_lens = [x_prompt.shape[1]] * x_prompt.shape[0] + [x_sample.shape[1]] * x_sample.shape[0]
    groups = ((0, tp), (tp, ts))
    bf16 = lambda w: w.astype(jnp.bfloat16)

    x = (x_prompt.reshape(tp, d), x_sample.reshape(ts, d))
    for i in range(depth):
        j = i // 2
        last = i == depth - 1
        if i % 2 == 0:
            proj = norm_matmul(x, norm_mix[i], bf16(ret_w_in[j]), jnp.bfloat16, PROJ_COLS)
            y_fwd = retention(proj, seq_lens, ret_decay_fwd[j], False, decay_reverse=ret_decay_bwd[j])
            z = retention(proj, seq_lens, ret_decay_bwd[j], True, y_fwd)
            w_out = ret_w_out[j]
        else:
            qkv = norm_matmul(x, norm_mix[i], bf16(att_w_qkv[j]), jnp.float32, PROJ_COLS)
            z = dilated_attention(qkv, seq_lens)
            w_out = att_w_out[j]
        x1, hn, aff = outproj_router(z, bf16(w_out), x, norm_ffn[i], moe_router[i])
        x = moe_layer(x1, hn, aff, groups, moe_w_gate, moe_w_up, moe_w_down, i,
                      norm_final, final=last, out_rows=(tp, ts) if last else (tp + ts,))
    return x[0].reshape(x_prompt.shape), x[1].reshape(x_sample.shape)
```
